```python
import math
import jax, jax.numpy as jnp
from jax import lax
import numpy as np

D_MODEL = 1024
BATCH = 8
SEQ = 4096
DEPTH = 1

HEAD_DIM = 64
MIX_WIDTH = D_MODEL
SB_WIDTH = MIX_WIDTH // 2
SB_HEADS = SB_WIDTH // HEAD_DIM
DF_V_DIM = 2 * HEAD_DIM
DF_WIDTH = MIX_WIDTH - SB_WIDTH
DF_HEADS = DF_WIDTH // DF_V_DIM
DF_QK_WIDTH = DF_HEADS * 2 * HEAD_DIM
IN_PROJ_WIDTH = 3 * SB_WIDTH + 2 * DF_QK_WIDTH + DF_WIDTH
SCALE = HEAD_DIM ** -0.5
Q_BLOCK = 128
ALIBI_SLOPES = tuple(2.0 ** (-8.0 * (h + 1) / DF_HEADS) for h in range(DF_HEADS))
N_GROUPS = 4
EXPERTS_PER_GROUP = 4
N_EXPERTS = N_GROUPS * EXPERTS_PER_GROUP
TOP_K_IN_GROUP = 2
D_EXPERT = D_MODEL // 2
PLE_DIM = 256
NORM_EPS = 1e-6

kernel_name = 'hymba_stickbreak_diffattn_hmoe_ple'


def _rmsnorm(x, g):
    xf = x.astype(jnp.float32)
    y = xf * lax.rsqrt(jnp.mean(xf * xf, axis=-1, keepdims=True) + NORM_EPS)
    return (y * g.astype(jnp.float32)).astype(x.dtype)


def _stick_breaking_block(q_blk, k_pre, v_pre, q_start):
    n_keys = k_pre.shape[2]
    z = jnp.einsum('bhqd,bhkd->bhqk', q_blk, k_pre).astype(jnp.float32) * SCALE
    t_pos = q_start + jnp.arange(Q_BLOCK)[:, None]
    s_pos = jnp.arange(n_keys)[None, :]
    strict = s_pos < t_pos
    log_beta = jax.nn.log_sigmoid(z)
    log_keep = jnp.where(strict, jax.nn.log_sigmoid(-z), 0.0)
    log_survive = lax.cumsum(log_keep, axis=3, reverse=True) - log_keep
    weights = jnp.where(strict, jnp.exp(log_beta + log_survive), 0.0)
    return jnp.einsum('bhqk,bhkd->bhqd', weights.astype(v_pre.dtype), v_pre)


def _differential_block(q_blk, k_pre, v_pre, q_start, lam, slopes):
    n_keys = k_pre.shape[3]
    z = jnp.einsum('bhcqd,bhckd->bhcqk', q_blk, k_pre).astype(jnp.float32) * SCALE
    t_pos = q_start + jnp.arange(Q_BLOCK)[:, None]
    s_pos = jnp.arange(n_keys)[None, :]
    dist = (t_pos - s_pos).astype(jnp.float32)
    z = z - slopes[None, :, None, None, None] * dist
    z = jnp.where(s_pos <= t_pos, z, -jnp.inf)
    probs = jax.nn.softmax(z, axis=-1)
    attn = probs[:, :, 0] - lam * probs[:, :, 1]
    return jnp.einsum('bhqk,bhkd->bhqd', attn.astype(v_pre.dtype), v_pre)


def _hybrid_layer(x, p_i, layer_idx, g_mix, w_in, lambda_q1, lambda_k1, lambda_q2,
                  lambda_k2, g_sb_out, g_df_out, w_out, g_ffn, w_router_group,
                  b_router_group, w_router_expert, b_router_expert, w_expert_gate,
                  w_expert_up, w_expert_down, g_ple, w_ple_gate, w_ple_proj):
    bsz, seq, _ = x.shape
    h = _rmsnorm(x, g_mix)
    proj = h @ w_in
    q_sb, k_sb, v_sb, q_df, k_df, v_df = jnp.split(
        proj, [SB_WIDTH, 2 * SB_WIDTH, 3 * SB_WIDTH,
               3 * SB_WIDTH + DF_QK_WIDTH, 3 * SB_WIDTH + 2 * DF_QK_WIDTH], axis=-1)
    q_sb = q_sb.reshape(bsz, seq, SB_HEADS, HEAD_DIM).transpose(0, 2, 1, 3)
    k_sb = k_sb.reshape(bsz, seq, SB_HEADS, HEAD_DIM).transpose(0, 2, 1, 3)
    v_sb = v_sb.reshape(bsz, seq, SB_HEADS, HEAD_DIM).transpose(0, 2, 1, 3)
    q_df = q_df.reshape(bsz, seq, DF_HEADS, 2, HEAD_DIM).transpose(0, 2, 3, 1, 4)
    k_df = k_df.reshape(bsz, seq, DF_HEADS, 2, HEAD_DIM).transpose(0, 2, 3, 1, 4)
    v_df = v_df.reshape(bsz, seq, DF_HEADS, DF_V_DIM).transpose(0, 2, 1, 3)

    lambda_init = 0.8 - 0.6 * math.exp(-0.3 * layer_idx)
    lam = (jnp.exp(jnp.sum(lambda_q1.astype(jnp.float32) * lambda_k1.astype(jnp.float32)))
           - jnp.exp(jnp.sum(lambda_q2.astype(jnp.float32) * lambda_k2.astype(jnp.float32)))
           + lambda_init)
    slopes = jnp.asarray(ALIBI_SLOPES, dtype=jnp.float32)

    sb_blocks, df_blocks = [], []
    for blk in range(seq // Q_BLOCK):
        qs = blk * Q_BLOCK
        qe = qs + Q_BLOCK
        sb_blocks.append(_stick_breaking_block(q_sb[:, :, qs:qe], k_sb[:, :, :qe],
                                               v_sb[:, :, :qe], qs))
        df_blocks.append(_differential_block(q_df[:, :, :, qs:qe], k_df[:, :, :, :qe],
                                             v_df[:, :, :qe], qs, lam, slopes))
    o_sb = jnp.concatenate(sb_blocks, axis=2).transpose(0, 2, 1, 3).reshape(bsz, seq, SB_WIDTH)
    o_sb = _rmsnorm(o_sb, g_sb_out)
    o_df = jnp.concatenate(df_blocks, axis=2)
    o_df = _rmsnorm(o_df, g_df_out) * (1.0 - lambda_init)
    o_df = o_df.transpose(0, 2, 1, 3).reshape(bsz, seq, DF_WIDTH)
    x = x + jnp.concatenate([o_sb, o_df], axis=-1) @ w_out

    h2 = _rmsnorm(x, g_ffn).reshape(-1, D_MODEL)
    group_logits = (h2 @ w_router_group).astype(jnp.float32) + b_router_group.astype(jnp.float32)
    group_probs = jax.nn.softmax(group_logits, axis=-1)
    g_idx = jnp.argmax(group_logits, axis=-1)
    g_w = jnp.take_along_axis(group_probs, g_idx[:, None], axis=1)[:, 0]
    expert_logits = ((h2 @ w_router_expert).astype(jnp.float32)
                     + b_router_expert.astype(jnp.float32)).reshape(-1, N_GROUPS, EXPERTS_PER_GROUP)
    in_group = jnp.take_along_axis(expert_logits, g_idx[:, None, None], axis=1)[:, 0]
    top_vals, top_idx = lax.top_k(in_group, TOP_K_IN_GROUP)
    top_w = jax.nn.softmax(top_vals, axis=-1) * g_w[:, None]
    expert_ids = g_idx[:, None] * EXPERTS_PER_GROUP + top_idx
    gates = jnp.sum(jax.nn.one_hot(expert_ids, N_EXPERTS, dtype=jnp.float32)
                    * top_w[..., None], axis=1).astype(h2.dtype)
    y = jnp.zeros_like(h2)
    for e in range(N_EXPERTS):
        hid = jax.nn.silu(h2 @ w_expert_gate[e]) * (h2 @ w_expert_up[e])
        y = y + gates[:, e:e + 1] * (hid @ w_expert_down[e])
    x = x + y.reshape(bsz, seq, D_MODEL)

    gate = jax.nn.sigmoid(_rmsnorm(x, g_ple) @ w_ple_gate)
    x = x + gate * (p_i @ w_ple_proj)
    return x


def setup_inputs(seed: int = 0) -> dict:
    key = jax.random.key(seed)
    ks = jax.random.split(key, 23)
    f32 = jnp.float32

    def nrm(k, shape, scale):
        return jax.random.normal(k, shape, f32) * scale

    def gain(k, shape):
        return 1.0 + 0.02 * jax.random.normal(k, shape, f32)

    return {
        'x': nrm(ks[0], (BATCH, SEQ, D_MODEL), 1.0),
        'p': nrm(ks[1], (DEPTH, BATCH, SEQ, PLE_DIM), 1.0),
        'g_mix': gain(ks[2], (DEPTH, D_MODEL)),
        'w_in': nrm(ks[3], (DEPTH, D_MODEL, IN_PROJ_WIDTH), D_MODEL ** -0.5),
        'lambda_q1': nrm(ks[4], (DEPTH, HEAD_DIM), 0.1),
        'lambda_k1': nrm(ks[5], (DEPTH, HEAD_DIM), 0.1),
        'lambda_q2': nrm(ks[6], (DEPTH, HEAD_DIM), 0.1),
        'lambda_k2': nrm(ks[7], (DEPTH, HEAD_DIM), 0.1),
        'g_sb_out': gain(ks[8], (DEPTH, SB_WIDTH)),
        'g_df_out': gain(ks[9], (DEPTH, DF_V_DIM)),
        'w_out': nrm(ks[10], (DEPTH, MIX_WIDTH, D_MODEL), MIX_WIDTH ** -0.5),
        'g_ffn': gain(ks[11], (DEPTH, D_MODEL)),
        'w_router_group': nrm(ks[12], (DEPTH, D_MODEL, N_GROUPS), D_MODEL ** -0.5),
        'b_router_group': nrm(ks[13], (DEPTH, N_GROUPS), 0.01),
        'w_router_expert': nrm(ks[14], (DEPTH, D_MODEL, N_EXPERTS), D_MODEL ** -0.5),
        'b_router_expert': nrm(ks[15], (DEPTH, N_EXPERTS), 0.01),
        'w_expert_gate': nrm(ks[16], (DEPTH, N_EXPERTS, D_MODEL, D_EXPERT), D_MODEL ** -0.5),
        'w_expert_up': nrm(ks[17], (DEPTH, N_EXPERTS, D_MODEL, D_EXPERT), D_MODEL ** -0.5),
        'w_expert_down': nrm(ks[18], (DEPTH, N_EXPERTS, D_EXPERT, D_MODEL), D_EXPERT ** -0.5),
        'g_ple': gain(ks[19], (DEPTH, D_MODEL)),
        'w_ple_gate': nrm(ks[20], (DEPTH, D_MODEL, D_MODEL), D_MODEL ** -0.5),
        'w_ple_proj': nrm(ks[21], (DEPTH, PLE_DIM, D_MODEL), PLE_DIM ** -0.5),
        'g_final': gain(ks[22], (D_MODEL,)),
    }


def reference(x, p, g_mix, w_in, lambda_q1, lambda_k1, lambda_q2, lambda_k2, g_sb_out,
              g_df_out, w_out, g_ffn, w_router_group, b_router_group, w_router_expert,
              b_router_expert, w_expert_gate, w_expert_up, w_expert_down, g_ple,
              w_ple_gate, w_ple_proj, g_final):
    for i in range(DEPTH):
        x = _hybrid_layer(x, p[i], i, g_mix[i], w_in[i], lambda_q1[i], lambda_k1[i],
                          lambda_q2[i], lambda_k2[i], g_sb_out[i], g_df_out[i], w_out[i],
                          g_ffn[i], w_router_group[i], b_router_group[i],
                          w_router_expert[i], b_router_expert[i], w_expert_gate[i],
                          w_expert_up[i], w_expert_down[i], g_ple[i], w_ple_gate[i],
                          w_ple_proj[i])
    return _rmsnorm(x, g_final)
```

```python
import functools
import math

import jax
import jax.numpy as jnp
from jax import lax
from jax.experimental import pallas as pl
from jax.experimental.pallas import tpu as pltpu

F32 = jnp.float32
BF16 = jnp.bfloat16

HEAD_DIM = 64
LANES = 128
N_SB_PAIRS = 4
N_DF_HEADS = 4
SB_WIDTH = 512
DF_WIDTH = 512
SCALE = HEAD_DIM ** -0.5
NORM_EPS = 1e-6
N_GROUPS = 4
EXPERTS_PER_GROUP = 4
N_EXPERTS = 16
ROUTER_LANE0 = N_GROUPS
ALIBI_SLOPES = tuple(2.0 ** (-8.0 * (h + 1) / N_DF_HEADS) for h in range(N_DF_HEADS))
SB_LOG_ZERO = -110.0
VMEM_LIMIT = 48 * 1024 * 1024


def _rms(x, g):
    ms = jnp.mean(x * x, axis=-1, keepdims=True)
    return x * lax.rsqrt(ms + NORM_EPS) * g


def _dot(a, b):
    return jnp.dot(a, b, preferred_element_type=F32)


def _dot_nt(a, b):
    return lax.dot_general(a, b, (((1,), (1,)), ((), ())), preferred_element_type=F32)


def _params(*sem):
    return pltpu.CompilerParams(dimension_semantics=sem, vmem_limit_bytes=VMEM_LIMIT)


def _inproj_kernel(x_ref, g_ref, w_ref, o_ref, *, tn):
    h = _rms(x_ref[...], g_ref[...]).astype(BF16)
    for j in range(o_ref.shape[1] // tn):
        o_ref[:, j * tn:(j + 1) * tn] = _dot(h, w_ref[:, j * tn:(j + 1) * tn]).astype(o_ref.dtype)


def _inproj(x2d, g, w_bf16, tm):
    n, d = x2d.shape
    width = w_bf16.shape[1]
    return pl.pallas_call(
        functools.partial(_inproj_kernel, tn=1024),
        grid=(n // tm,),
        in_specs=[pl.BlockSpec((tm, d), lambda i: (i, 0)),
                  pl.BlockSpec((1, d), lambda i: (0, 0)),
                  pl.BlockSpec((d, width), lambda i: (0, 0))],
        out_specs=pl.BlockSpec((tm, width), lambda i: (i, 0)),
        out_shape=jax.ShapeDtypeStruct((n, width), BF16),
        compiler_params=_params("parallel"),
        name="inproj",
    )(x2d, g, w_bf16)


def _sb_kernel(q_ref, k_ref, v_ref, u_ref, o_ref, acc_ref, c_ref, *, t):
    qi = pl.program_id(2)
    lane = lax.broadcasted_iota(jnp.int32, (t, LANES), 1)
    first = lane < HEAD_DIM
    q = q_ref[0] * SCALE
    zero = jnp.zeros_like(q)
    q2 = jnp.concatenate([jnp.where(first, q, zero), jnp.where(first, zero, q)], axis=0)

    def block(kb, strict_mask):
        start = pl.multiple_of(kb * t, t)
        k = k_ref[0, pl.ds(start, t), :]
        v = v_ref[0, pl.ds(start, t), :]
        z = _dot_nt(q2, k)
        soft = jnp.log1p(jnp.exp(-jnp.abs(z)))
        log_beta = jnp.minimum(z, 0.0) - soft
        log_keep = -jnp.maximum(z, 0.0) - soft
        if strict_mask is not None:
            log_keep = jnp.where(strict_mask, log_keep, 0.0)
        hi = log_keep.astype(BF16)
        lo = (log_keep - hi.astype(F32)).astype(BF16)
        rev = _dot(jnp.concatenate([hi, lo], axis=1), u_ref[...])
        c = c_ref[...]
        w = jnp.exp(log_beta + rev + c)
        if strict_mask is not None:
            w = jnp.where(strict_mask, w, 0.0)
        w = w.astype(BF16)
        vz = jnp.zeros_like(v)
        v2 = jnp.concatenate([jnp.where(first, v, vz), jnp.where(first, vz, v)], axis=0)
        acc_ref[...] += _dot(jnp.concatenate([w[:t], w[t:]], axis=1), v2)
        c_new = c + jnp.sum(log_keep, axis=-1, keepdims=True)
        c_ref[...] = c_new
        return jnp.max(c_new)

    acc_ref[...] = jnp.zeros_like(acc_ref)
    c_ref[...] = jnp.zeros_like(c_ref)
    row = lax.broadcasted_iota(jnp.int32, (2 * t, t), 0)
    col = lax.broadcasted_iota(jnp.int32, (2 * t, t), 1)
    strict = col < jnp.where(row >= t, row - t, row)
    cmax = block(qi, strict)

    def cond(carry):
        kb, cm = carry
        return jnp.logical_and(kb >= 0, cm > SB_LOG_ZERO)

    def body(carry):
        kb, _ = carry
        return kb - 1, block(kb, None)

    lax.while_loop(cond, body, (qi - 1, cmax))
    o_ref[0] = acc_ref[...]


def _sb_attention(proj3, u2, t):
    b, s, _ = proj3.shape
    return pl.pallas_call(
        functools.partial(_sb_kernel, t=t),
        grid=(b, N_SB_PAIRS, s // t),
        in_specs=[pl.BlockSpec((1, t, LANES), lambda bi, hp, qi: (bi, qi, hp)),
                  pl.BlockSpec((1, s, LANES), lambda bi, hp, qi: (bi, 0, N_SB_PAIRS + hp)),
                  pl.BlockSpec((1, s, LANES), lambda bi, hp, qi: (bi, 0, 2 * N_SB_PAIRS + hp)),
                  pl.BlockSpec((2 * t, t), lambda bi, hp, qi: (0, 0))],
        out_specs=pl.BlockSpec((1, t, LANES), lambda bi, hp, qi: (bi, qi, hp)),
        out_shape=jax.ShapeDtypeStruct((b, s, SB_WIDTH), F32),
        scratch_shapes=[pltpu.VMEM((t, LANES), F32), pltpu.VMEM((2 * t, 1), F32)],
        compiler_params=_params("parallel", "parallel", "arbitrary"),
        name="sb_attention",
    )(proj3, proj3, proj3, u2)


def _df_kernel(q_ref, k_ref, v_ref, lam_ref, g_ref, o_ref, acc_ref, m_ref, l_ref, *, t, lambda_init):
    h = pl.program_id(1)
    qi = pl.program_id(2)
    slope = jnp.float32(ALIBI_SLOPES[-1])
    for idx in range(N_DF_HEADS - 2, -1, -1):
        slope = jnp.where(h == idx, jnp.float32(ALIBI_SLOPES[idx]), slope)
    lane = lax.broadcasted_iota(jnp.int32, (t, LANES), 1)
    first = lane < HEAD_DIM
    q = q_ref[0] * SCALE
    zero = jnp.zeros_like(q)
    q2 = jnp.concatenate([jnp.where(first, q, zero), jnp.where(first, zero, q)], axis=0)
    col_f = lax.broadcasted_iota(jnp.int32, (1, t), 1).astype(F32)

    def block(kb, causal_mask):
        start = pl.multiple_of(kb * t, t)
        k = k_ref[0, pl.ds(start, t), :]
        v = v_ref[0, pl.ds(start, t), :]
        bias = slope * (col_f + ((kb - qi) * t).astype(F32))
        z = _dot_nt(q2, k) + bias
        if causal_mask is not None:
            z = jnp.where(causal_mask, z, -jnp.inf)
        m_old = m_ref[...]
        m_new = jnp.maximum(m_old, jnp.max(z, axis=-1, keepdims=True))
        alpha = jnp.exp(m_old - m_new)
        p = jnp.exp(z - m_new)
        l_ref[...] = alpha * l_ref[...] + jnp.sum(p, axis=-1, keepdims=True)
        acc_ref[...] = alpha * acc_ref[...] + _dot(p.astype(BF16), v)
        m_ref[...] = m_new

    acc_ref[...] = jnp.zeros_like(acc_ref)
    l_ref[...] = jnp.zeros_like(l_ref)
    m_ref[...] = jnp.full_like(m_ref, -jnp.inf)

    row = lax.broadcasted_iota(jnp.int32, (2 * t, t), 0)
    col = lax.broadcasted_iota(jnp.int32, (2 * t, t), 1)
    causal = col <= jnp.where(row >= t, row - t, row)
    block(qi, causal)

    def body(kb, carry):
        block(kb, None)
        return carry

    lax.fori_loop(0, qi, body, 0)

    lam_vec = lam_ref[...]
    lam = (jnp.exp(jnp.sum(lam_vec[0:1] * lam_vec[1:2], axis=-1, keepdims=True))
           - jnp.exp(jnp.sum(lam_vec[2:3] * lam_vec[3:4], axis=-1, keepdims=True))
           + lambda_init)
    acc = acc_ref[...]
    inv_l = 1.0 / l_ref[...]
    o = acc[:t] * inv_l[:t] - lam * (acc[t:] * inv_l[t:])
    o_ref[0] = (_rms(o, g_ref[...]) * (1.0 - lambda_init)).astype(o_ref.dtype)


def _df_attention(proj3, lam_rows, g_df, t, lambda_init):
    b, s, _ = proj3.shape
    col0 = 3 * N_SB_PAIRS
    return pl.pallas_call(
        functools.partial(_df_kernel, t=t, lambda_init=lambda_init),
        grid=(b, N_DF_HEADS, s // t),
        in_specs=[pl.BlockSpec((1, t, LANES), lambda bi, h, qi: (bi, qi, col0 + h)),
                  pl.BlockSpec((1, s, LANES), lambda bi, h, qi: (bi, 0, col0 + N_DF_HEADS + h)),
                  pl.BlockSpec((1, s, LANES), lambda bi, h, qi: (bi, 0, col0 + 2 * N_DF_HEADS + h)),
                  pl.BlockSpec((4, HEAD_DIM), lambda bi, h, qi: (0, 0)),
                  pl.BlockSpec((1, LANES), lambda bi, h, qi: (0, 0))],
        out_specs=pl.BlockSpec((1, t, LANES), lambda bi, h, qi: (bi, qi, h)),
        out_shape=jax.ShapeDtypeStruct((b, s, DF_WIDTH), BF16),
        scratch_shapes=[pltpu.VMEM((2 * t, LANES), F32), pltpu.VMEM((2 * t, 1), F32),
                        pltpu.VMEM((2 * t, 1), F32)],
        compiler_params=_params("parallel", "parallel", "arbitrary"),
        name="df_attention",
    )(proj3, proj3, proj3, lam_rows, g_df)


def _split_bf16(x):
    hi = x.astype(BF16)
    return hi, (x - hi.astype(F32)).astype(BF16)


def _route(logits):
    lane = lax.broadcasted_iota(jnp.int32, logits.shape, 1)
    neg = -jnp.inf
    big = jnp.int32(LANES)
    gl = jnp.where(lane < N_GROUPS, logits, neg)
    gmax = jnp.max(gl, axis=-1, keepdims=True)
    g_idx = jnp.min(jnp.where(gl == gmax, lane, big), axis=-1, keepdims=True)
    g_w = 1.0 / jnp.sum(jnp.exp(gl - gmax), axis=-1, keepdims=True)
    e_lane = lane - ROUTER_LANE0
    in_group = jnp.logical_and(jnp.logical_and(e_lane >= 0, e_lane < N_EXPERTS),
                               (e_lane // EXPERTS_PER_GROUP) == g_idx)
    v1 = jnp.where(in_group, logits, neg)
    t1 = jnp.max(v1, axis=-1, keepdims=True)
    i1 = jnp.min(jnp.where(v1 == t1, lane, big), axis=-1, keepdims=True)
    v2 = jnp.where(lane == i1, neg, v1)
    t2 = jnp.max(v2, axis=-1, keepdims=True)
    i2 = jnp.min(jnp.where(v2 == t2, lane, big), axis=-1, keepdims=True)
    e2 = jnp.exp(t2 - t1)
    w1 = g_w / (1.0 + e2)
    w2 = w1 * e2
    return jnp.where(lane == i1, w1, jnp.where(lane == i2, w2, 0.0))


def _outproj_kernel(osb_ref, odf_ref, x_ref, gsb_ref, wsb_ref, wdf_ref, gffn_ref, wr_ref, br_ref,
                    x1_ref, h2_ref, gates_ref):
    a_sb = _rms(osb_ref[...], gsb_ref[...]).astype(BF16)
    x1 = x_ref[...] + _dot(a_sb, wsb_ref[...]) + _dot(odf_ref[...], wdf_ref[...])
    x1_ref[...] = x1
    h2 = _rms(x1, gffn_ref[...])
    h2_ref[...] = h2.astype(BF16)
    hi, lo = _split_bf16(h2)
    logits = _dot(jnp.concatenate([hi, hi, lo], axis=1), wr_ref[...]) + br_ref[...]
    gates_ref[...] = _route(logits)


def _outproj(o_sb, o_df, x2d, g_sb, w_sb, w_df, g_ffn, w_router3, b_router, tm):
    n, d = x2d.shape
    row = lambda i: (i, 0)
    const = lambda i: (0, 0)
    return pl.pallas_call(
        _outproj_kernel,
        grid=(n // tm,),
        in_specs=[pl.BlockSpec((tm, SB_WIDTH), row), pl.BlockSpec((tm, DF_WIDTH), row),
                  pl.BlockSpec((tm, d), row), pl.BlockSpec((1, SB_WIDTH), const),
                  pl.BlockSpec((SB_WIDTH, d), const), pl.BlockSpec((DF_WIDTH, d), const),
                  pl.BlockSpec((1, d), const), pl.BlockSpec((3 * d, LANES), const),
                  pl.BlockSpec((1, LANES), const)],
        out_specs=[pl.BlockSpec((tm, d), row), pl.BlockSpec((tm, d), row),
                   pl.BlockSpec((tm, LANES), row)],
        out_shape=[jax.ShapeDtypeStruct((n, d), F32), jax.ShapeDtypeStruct((n, d), BF16),
                   jax.ShapeDtypeStruct((n, LANES), F32)],
        compiler_params=_params("parallel"),
        name="outproj_router",
    )(o_sb, o_df, x2d, g_sb, w_sb, w_df, g_ffn, w_router3, b_router)


def _moe_kernel(h_ref, gates_ref, x1_ref, wg_ref, wu_ref, wd_ref, o_ref):
    e = pl.program_id(1)

    @pl.when(e == 0)
    def _():
        o_ref[...] = x1_ref[...]

    h = h_ref[...]
    gates = gates_ref[...]
    lane = lax.broadcasted_iota(jnp.int32, gates.shape, 1)
    g_e = jnp.sum(jnp.where(lane == e + ROUTER_LANE0, gates, 0.0), axis=-1, keepdims=True)
    a = _dot(h, wg_ref[0])
    hid = (a * jax.nn.sigmoid(a)) * _dot(h, wu_ref[0])
    o_ref[...] += g_e * _dot(hid.astype(BF16), wd_ref[0])


def _moe(h2, gates, x1, wg, wu, wd, tm):
    n, d = x1.shape
    de = wg.shape[2]
    row = lambda i, e: (i, 0)
    return pl.pallas_call(
        _moe_kernel,
        grid=(n // tm, N_EXPERTS),
        in_specs=[pl.BlockSpec((tm, d), row), pl.BlockSpec((tm, LANES), row), pl.BlockSpec((tm, d), row),
                  pl.BlockSpec((1, d, de), lambda i, e: (e, 0, 0)),
                  pl.BlockSpec((1, d, de), lambda i, e: (e, 0, 0)),
                  pl.BlockSpec((1, de, d), lambda i, e: (e, 0, 0))],
        out_specs=pl.BlockSpec((tm, d), row),
        out_shape=jax.ShapeDtypeStruct((n, d), F32),
        compiler_params=_params("parallel", "arbitrary"),
        name="moe",
    )(h2, gates, x1, wg, wu, wd)


def _ple_kernel(x_ref, p_ref, gple_ref, wgate_ref, wproj_ref, gfin_ref, o_ref):
    x = x_ref[...]
    gate = jax.nn.sigmoid(_dot(_rms(x, gple_ref[...]).astype(BF16), wgate_ref[...]))
    x3 = x + gate * _dot(p_ref[...].astype(BF16), wproj_ref[...])
    o_ref[...] = _rms(x3, gfin_ref[...])


def _ple(x2, p2d, g_ple, w_gate, w_proj, g_final, tm):
    n, d = x2.shape
    pd = p2d.shape[1]
    row = lambda i: (i, 0)
    const = lambda i: (0, 0)
    return pl.pallas_call(
        _ple_kernel,
        grid=(n // tm,),
        in_specs=[pl.BlockSpec((tm, d), row), pl.BlockSpec((tm, pd), row), pl.BlockSpec((1, d), const),
                  pl.BlockSpec((d, d), const), pl.BlockSpec((pd, d), const), pl.BlockSpec((1, d), const)],
        out_specs=pl.BlockSpec((tm, d), row),
        out_shape=jax.ShapeDtypeStruct((n, d), F32),
        compiler_params=_params("parallel"),
        name="ple_final",
    )(x2, p2d, g_ple, w_gate, w_proj, g_final)


def _layer(x, p_i, layer_idx, g_mix, w_in, lambda_q1, lambda_k1, lambda_q2, lambda_k2, g_sb_out, g_df_out,
           w_out, g_ffn, w_router_group, b_router_group, w_router_expert, b_router_expert, w_expert_gate,
           w_expert_up, w_expert_down, g_ple, w_ple_gate, w_ple_proj, g_out):
    b, s, d = x.shape
    n = b * s
    tm = min(512, n)
    t = min(256, s)
    x2d = x.reshape(n, d)
    lambda_init = 0.8 - 0.6 * math.exp(-0.3 * layer_idx)

    proj = _inproj(x2d, g_mix.reshape(1, d), w_in.astype(BF16), tm)
    proj3 = proj.reshape(b, s, proj.shape[1])

    tri = (lax.broadcasted_iota(jnp.int32, (t, t), 0) > lax.broadcasted_iota(jnp.int32, (t, t), 1))
    u2 = jnp.concatenate([tri, tri], axis=0).astype(BF16)
    o_sb = _sb_attention(proj3, u2, t).reshape(n, SB_WIDTH)

    lam_rows = jnp.stack([lambda_q1, lambda_k1, lambda_q2, lambda_k2]).astype(F32)
    o_df = _df_attention(proj3, lam_rows, g_df_out.reshape(1, LANES).astype(F32), t,
                         lambda_init).reshape(n, DF_WIDTH)

    w_router = jnp.concatenate([w_router_group, w_router_expert], axis=1).astype(F32)
    w_router = jnp.pad(w_router, ((0, 0), (0, LANES - w_router.shape[1])))
    wr_hi, wr_lo = _split_bf16(w_router)
    b_router = jnp.pad(jnp.concatenate([b_router_group, b_router_expert]).astype(F32),
                       (0, LANES - N_GROUPS - N_EXPERTS)).reshape(1, LANES)
    w_out_bf = w_out.astype(BF16)
    x1, h2, gates = _outproj(o_sb, o_df, x2d, g_sb_out.reshape(1, SB_WIDTH), w_out_bf[:SB_WIDTH],
                             w_out_bf[SB_WIDTH:], g_ffn.reshape(1, d),
                             jnp.concatenate([wr_hi, wr_lo, wr_hi], axis=0), b_router, tm)

    x2 = _moe(h2, gates, x1, w_expert_gate.astype(BF16), w_expert_up.astype(BF16),
              w_expert_down.astype(BF16), tm)

    out = _ple(x2, p_i.reshape(n, p_i.shape[-1]), g_ple.reshape(1, d), w_ple_gate.astype(BF16),
               w_ple_proj.astype(BF16), g_out.reshape(1, d), tm)
    return out.reshape(b, s, d)


def kernel(x, p, g_mix, w_in, lambda_q1, lambda_k1, lambda_q2, lambda_k2, g_sb_out, g_df_out, w_out, g_ffn,
           w_router_group, b_router_group, w_router_expert, b_router_expert, w_expert_gate, w_expert_up,
           w_expert_down, g_ple, w_ple_gate, w_ple_proj, g_final):
    depth = p.shape[0]
    assert depth == 1, "the final norm is fused into the single layer's last kernel"
    return _layer(x, p[0], 0, g_mix[0], w_in[0], lambda_q1[0], lambda_k1[0], lambda_q2[0], lambda_k2[0],
                  g_sb_out[0], g_df_out[0], w_out[0], g_ffn[0], w_router_group[0], b_router_group[0],
                  w_router_expert[0], b_router_expert[0], w_expert_gate[0], w_expert_up[0],
                  w_expert_down[0], g_ple[0], w_ple_gate[0], w_ple_proj[0], g_final)
```

```python
import functools
import math

import jax
import jax.numpy as jnp
from jax import lax
from jax.experimental import pallas as pl
from jax.experimental.pallas import tpu as pltpu

F32 = jnp.float32
BF16 = jnp.bfloat16

HEAD_DIM = 64
LANES = 128
N_SB_PAIRS = 4
N_DF_HEADS = 4
SB_WIDTH = 512
DF_WIDTH = 512
SCALE = HEAD_DIM ** -0.5
NORM_EPS = 1e-6
N_GROUPS = 4
EXPERTS_PER_GROUP = 4
N_EXPERTS = 16
ROUTER_LANE0 = N_GROUPS
ALIBI_SLOPES = tuple(2.0 ** (-8.0 * (h + 1) / N_DF_HEADS) for h in range(N_DF_HEADS))
SB_LOG_ZERO = -110.0
VMEM_LIMIT = 48 * 1024 * 1024


def _rms(x, g):
    ms = jnp.mean(x * x, axis=-1, keepdims=True)
    return x * lax.rsqrt(ms + NORM_EPS) * g


def _dot(a, b):
    return jnp.dot(a, b, preferred_element_type=F32)


def _dot_nt(a, b):
    return lax.dot_general(a, b, (((1,), (1,)), ((), ())), preferred_element_type=F32)


def _params(*sem):
    return pltpu.CompilerParams(dimension_semantics=sem, vmem_limit_bytes=VMEM_LIMIT)


def _inproj_kernel(x_ref, g_ref, w_ref, o_ref, *, tn):
    h = _rms(x_ref[...], g_ref[...]).astype(BF16)
    for j in range(o_ref.shape[1] // tn):
        o_ref[:, j * tn:(j + 1) * tn] = _dot(h, w_ref[:, j * tn:(j + 1) * tn]).astype(o_ref.dtype)


def _inproj(x2d, g, w_bf16, tm):
    n, d = x2d.shape
    width = w_bf16.shape[1]
    return pl.pallas_call(
        functools.partial(_inproj_kernel, tn=1024),
        grid=(n // tm,),
        in_specs=[pl.BlockSpec((tm, d), lambda i: (i, 0)),
                  pl.BlockSpec((1, d), lambda i: (0, 0)),
                  pl.BlockSpec((d, width), lambda i: (0, 0))],
        out_specs=pl.BlockSpec((tm, width), lambda i: (i, 0)),
        out_shape=jax.ShapeDtypeStruct((n, width), BF16),
        compiler_params=_params("parallel"),
        name="inproj",
    )(x2d, g, w_bf16)


def _sb_kernel(q_ref, k_ref, v_ref, u_ref, o_ref, acc_ref, c_ref, *, t):
    qi = pl.program_id(2)
    lane = lax.broadcasted_iota(jnp.int32, (t, LANES), 1)
    first = lane < HEAD_DIM
    q = q_ref[0] * SCALE
    zero = jnp.zeros_like(q)
    q2 = jnp.concatenate([jnp.where(first, q, zero), jnp.where(first, zero, q)], axis=0)

    def block(kb, strict_mask):
        start = pl.multiple_of(kb * t, t)
        k = k_ref[0, pl.ds(start, t), :]
        v = v_ref[0, pl.ds(start, t), :]
        z = _dot_nt(q2, k)
        soft = jnp.log1p(jnp.exp(-jnp.abs(z)))
        log_beta = jnp.minimum(z, 0.0) - soft
        log_keep = -jnp.maximum(z, 0.0) - soft
        if strict_mask is not None:
            log_keep = jnp.where(strict_mask, log_keep, 0.0)
        hi = log_keep.astype(BF16)
        lo = (log_keep - hi.astype(F32)).astype(BF16)
        rev = _dot(jnp.concatenate([hi, lo], axis=1), u_ref[...])
        c = c_ref[...]
        w = jnp.exp(log_beta + rev + c)
        if strict_mask is not None:
            w = jnp.where(strict_mask, w, 0.0)
        w = w.astype(BF16)
        vz = jnp.zeros_like(v)
        v2 = jnp.concatenate([jnp.where(first, v, vz), jnp.where(first, vz, v)], axis=0)
        acc_ref[...] += _dot(jnp.concatenate([w[:t], w[t:]], axis=1), v2)
        c_new = c + jnp.sum(log_keep, axis=-1, keepdims=True)
        c_ref[...] = c_new
        return jnp.max(c_new)

    acc_ref[...] = jnp.zeros_like(acc_ref)
    c_ref[...] = jnp.zeros_like(c_ref)
    row = lax.broadcasted_iota(jnp.int32, (2 * t, t), 0)
    col = lax.broadcasted_iota(jnp.int32, (2 * t, t), 1)
    strict = col < jnp.where(row >= t, row - t, row)
    cmax = block(qi, strict)

    def cond(carry):
        kb, cm = carry
        return jnp.logical_and(kb >= 0, cm > SB_LOG_ZERO)

    def body(carry):
        kb, _ = carry
        return kb - 1, block(kb, None)

    lax.while_loop(cond, body, (qi - 1, cmax))
    o_ref[0] = acc_ref[...]


def _sb_attention(proj3, u2, t):
    b, s, _ = proj3.shape
    return pl.pallas_call(
        functools.partial(_sb_kernel, t=t),
        grid=(b, N_SB_PAIRS, s // t),
        in_specs=[pl.BlockSpec((1, t, LANES), lambda bi, hp, qi: (bi, qi, hp)),
                  pl.BlockSpec((1, s, LANES), lambda bi, hp, qi: (bi, 0, N_SB_PAIRS + hp)),
                  pl.BlockSpec((1, s, LANES), lambda bi, hp, qi: (bi, 0, 2 * N_SB_PAIRS + hp)),
                  pl.BlockSpec((2 * t, t), lambda bi, hp, qi: (0, 0))],
        out_specs=pl.BlockSpec((1, t, LANES), lambda bi, hp, qi: (bi, qi, hp)),
        out_shape=jax.ShapeDtypeStruct((b, s, SB_WIDTH), F32),
        scratch_shapes=[pltpu.VMEM((t, LANES), F32), pltpu.VMEM((2 * t, 1), F32)],
        compiler_params=_params("parallel", "parallel", "arbitrary"),
        name="sb_attention",
    )(proj3, proj3, proj3, u2)


def _df_kernel(q_ref, k_ref, v_ref, lam_ref, g_ref, o_ref, acc_ref, m_ref, *, t, lambda_init):
    h = pl.program_id(1)
    qi = pl.program_id(2)
    slope = jnp.float32(ALIBI_SLOPES[-1])
    for idx in range(N_DF_HEADS - 2, -1, -1):
        slope = jnp.where(h == idx, jnp.float32(ALIBI_SLOPES[idx]), slope)
    lane = lax.broadcasted_iota(jnp.int32, (t, LANES), 1)
    row = lax.broadcasted_iota(jnp.int32, (t, LANES), 0)
    first = lane < HEAD_DIM
    q = q_ref[0] * SCALE
    zero = jnp.zeros_like(q)
    bias_on = jnp.where(lane < 2, 1.0, 0.0).astype(BF16)
    q2 = jnp.concatenate([jnp.concatenate([jnp.where(first, q, zero), bias_on], axis=1),
                          jnp.concatenate([jnp.where(first, zero, q), bias_on], axis=1)], axis=0)
    key_lo = jnp.where(lane == 0, ((row >> 8) << 8).astype(F32),
                       jnp.where(lane == 1, (row & 255).astype(F32), 0.0)) * slope
    lane0 = lane == 0
    ones_v = jnp.ones((t, LANES), BF16)

    def block(kb, causal_mask):
        start = pl.multiple_of(kb * t, t)
        k = k_ref[0, pl.ds(start, t), :]
        v = v_ref[0, pl.ds(start, t), :]
        offset = slope * ((kb - qi) * t).astype(F32)
        k_bias = (key_lo + jnp.where(lane0, offset, 0.0)).astype(BF16)
        z = _dot_nt(q2, jnp.concatenate([k, k_bias], axis=1))
        if causal_mask is not None:
            z = jnp.where(causal_mask, z, -jnp.inf)
        m_old = m_ref[...]
        m_new = jnp.maximum(m_old, jnp.max(z, axis=-1, keepdims=True))
        alpha = jnp.exp(m_old - m_new)
        p = jnp.exp(z - pltpu.repeat(m_new, t // LANES, axis=1)).astype(BF16)
        pv = _dot(p, jnp.concatenate([v, ones_v], axis=1))
        acc_ref[...] = jnp.concatenate([alpha, alpha], axis=1) * acc_ref[...] + pv
        m_ref[...] = m_new

    acc_ref[...] = jnp.zeros_like(acc_ref)
    m_ref[...] = jnp.full_like(m_ref, -jnp.inf)
    r2 = lax.broadcasted_iota(jnp.int32, (2 * t, t), 0)
    c2 = lax.broadcasted_iota(jnp.int32, (2 * t, t), 1)
    block(qi, c2 <= jnp.where(r2 >= t, r2 - t, r2))

    def body(kb, carry):
        block(kb, None)
        return carry

    lax.fori_loop(0, qi, body, 0)

    lam_vec = lam_ref[...]
    lam = (jnp.exp(jnp.sum(lam_vec[0:1] * lam_vec[1:2], axis=-1, keepdims=True))
           - jnp.exp(jnp.sum(lam_vec[2:3] * lam_vec[3:4], axis=-1, keepdims=True))
           + lambda_init)
    acc = acc_ref[...]
    ratio = acc[:, :LANES] / acc[:, LANES:]
    o = ratio[:t] - lam * ratio[t:]
    o_ref[0] = (_rms(o, g_ref[...]) * (1.0 - lambda_init)).astype(o_ref.dtype)


def _df_attention(proj3, lam_rows, g_df, t, lambda_init):
    b, s, _ = proj3.shape
    col0 = 3 * N_SB_PAIRS
    return pl.pallas_call(
        functools.partial(_df_kernel, t=t, lambda_init=lambda_init),
        grid=(b, N_DF_HEADS, s // t),
        in_specs=[pl.BlockSpec((1, t, LANES), lambda bi, h, qi: (bi, qi, col0 + h)),
                  pl.BlockSpec((1, s, LANES), lambda bi, h, qi: (bi, 0, col0 + N_DF_HEADS + h)),
                  pl.BlockSpec((1, s, LANES), lambda bi, h, qi: (bi, 0, col0 + 2 * N_DF_HEADS + h)),
                  pl.BlockSpec((4, HEAD_DIM), lambda bi, h, qi: (0, 0)),
                  pl.BlockSpec((1, LANES), lambda bi, h, qi: (0, 0))],
        out_specs=pl.BlockSpec((1, t, LANES), lambda bi, h, qi: (bi, qi, h)),
        out_shape=jax.ShapeDtypeStruct((b, s, DF_WIDTH), BF16),
        scratch_shapes=[pltpu.VMEM((2 * t, 2 * LANES), F32), pltpu.VMEM((2 * t, LANES), F32)],
        compiler_params=_params("parallel", "parallel", "arbitrary"),
        name="df_attention",
    )(proj3, proj3, proj3, lam_rows, g_df)


def _split_bf16(x):
    hi = x.astype(BF16)
    return hi, (x - hi.astype(F32)).astype(BF16)


def _route(logits):
    lane = lax.broadcasted_iota(jnp.int32, logits.shape, 1)
    neg = -jnp.inf
    big = jnp.int32(LANES)
    gl = jnp.where(lane < N_GROUPS, logits, neg)
    gmax = jnp.max(gl, axis=-1, keepdims=True)
    g_idx = jnp.min(jnp.where(gl == gmax, lane, big), axis=-1, keepdims=True)
    g_w = 1.0 / jnp.sum(jnp.exp(gl - gmax), axis=-1, keepdims=True)
    e_lane = lane - ROUTER_LANE0
    in_group = jnp.logical_and(jnp.logical_and(e_lane >= 0, e_lane < N_EXPERTS),
                               (e_lane // EXPERTS_PER_GROUP) == g_idx)
    v1 = jnp.where(in_group, logits, neg)
    t1 = jnp.max(v1, axis=-1, keepdims=True)
    i1 = jnp.min(jnp.where(v1 == t1, lane, big), axis=-1, keepdims=True)
    v2 = jnp.where(lane == i1, neg, v1)
    t2 = jnp.max(v2, axis=-1, keepdims=True)
    i2 = jnp.min(jnp.where(v2 == t2, lane, big), axis=-1, keepdims=True)
    e2 = jnp.exp(t2 - t1)
    w1 = g_w / (1.0 + e2)
    w2 = w1 * e2
    return jnp.where(lane == i1, w1, jnp.where(lane == i2, w2, 0.0))


def _outproj_kernel(osb_ref, odf_ref, x_ref, gsb_ref, wsb_ref, wdf_ref, gffn_ref, wr_ref, br_ref,
                    x1_ref, h2_ref, gates_ref):
    a_sb = _rms(osb_ref[...], gsb_ref[...]).astype(BF16)
    x1 = x_ref[...] + _dot(a_sb, wsb_ref[...]) + _dot(odf_ref[...], wdf_ref[...])
    x1_ref[...] = x1
    h2 = _rms(x1, gffn_ref[...])
    h2_ref[...] = h2.astype(BF16)
    hi, lo = _split_bf16(h2)
    logits = _dot(jnp.concatenate([hi, hi, lo], axis=1), wr_ref[...]) + br_ref[...]
    gates_ref[...] = _route(logits)


def _outproj(o_sb, o_df, x2d, g_sb, w_sb, w_df, g_ffn, w_router3, b_router, tm):
    n, d = x2d.shape
    row = lambda i: (i, 0)
    const = lambda i: (0, 0)
    return pl.pallas_call(
        _outproj_kernel,
        grid=(n // tm,),
        in_specs=[pl.BlockSpec((tm, SB_WIDTH), row), pl.BlockSpec((tm, DF_WIDTH), row),
                  pl.BlockSpec((tm, d), row), pl.BlockSpec((1, SB_WIDTH), const),
                  pl.BlockSpec((SB_WIDTH, d), const), pl.BlockSpec((DF_WIDTH, d), const),
                  pl.BlockSpec((1, d), const), pl.BlockSpec((3 * d, LANES), const),
                  pl.BlockSpec((1, LANES), const)],
        out_specs=[pl.BlockSpec((tm, d), row), pl.BlockSpec((tm, d), row),
                   pl.BlockSpec((tm, LANES), row)],
        out_shape=[jax.ShapeDtypeStruct((n, d), F32), jax.ShapeDtypeStruct((n, d), BF16),
                   jax.ShapeDtypeStruct((n, LANES), F32)],
        compiler_params=_params("parallel"),
        name="outproj_router",
    )(o_sb, o_df, x2d, g_sb, w_sb, w_df, g_ffn, w_router3, b_router)


def _moe_kernel(h_ref, gates_ref, x1_ref, wg_ref, wu_ref, wd_ref, o_ref):
    e = pl.program_id(1)

    @pl.when(e == 0)
    def _():
        o_ref[...] = x1_ref[...]

    h = h_ref[...]
    gates = gates_ref[...]
    lane = lax.broadcasted_iota(jnp.int32, gates.shape, 1)
    g_e = jnp.sum(jnp.where(lane == e + ROUTER_LANE0, gates, 0.0), axis=-1, keepdims=True)
    a = _dot(h, wg_ref[0])
    hid = (a * jax.nn.sigmoid(a)) * _dot(h, wu_ref[0])
    o_ref[...] += g_e * _dot(hid.astype(BF16), wd_ref[0])


def _moe(h2, gates, x1, wg, wu, wd, tm):
    n, d = x1.shape
    de = wg.shape[2]
    row = lambda i, e: (i, 0)
    return pl.pallas_call(
        _moe_kernel,
        grid=(n // tm, N_EXPERTS),
        in_specs=[pl.BlockSpec((tm, d), row), pl.BlockSpec((tm, LANES), row), pl.BlockSpec((tm, d), row),
                  pl.BlockSpec((1, d, de), lambda i, e: (e, 0, 0)),
                  pl.BlockSpec((1, d, de), lambda i, e: (e, 0, 0)),
                  pl.BlockSpec((1, de, d), lambda i, e: (e, 0, 0))],
        out_specs=pl.BlockSpec((tm, d), row),
        out_shape=jax.ShapeDtypeStruct((n, d), F32),
        compiler_params=_params("parallel", "arbitrary"),
        name="moe",
    )(h2, gates, x1, wg, wu, wd)


def _ple_kernel(x_ref, p_ref, gple_ref, wgate_ref, wproj_ref, gfin_ref, o_ref):
    x = x_ref[...]
    gate = jax.nn.sigmoid(_dot(_rms(x, gple_ref[...]).astype(BF16), wgate_ref[...]))
    x3 = x + gate * _dot(p_ref[...].astype(BF16), wproj_ref[...])
    o_ref[...] = _rms(x3, gfin_ref[...])


def _ple(x2, p2d, g_ple, w_gate, w_proj, g_final, tm):
    n, d = x2.shape
    pd = p2d.shape[1]
    row = lambda i: (i, 0)
    const = lambda i: (0, 0)
    return pl.pallas_call(
        _ple_kernel,
        grid=(n // tm,),
        in_specs=[pl.BlockSpec((tm, d), row), pl.BlockSpec((tm, pd), row), pl.BlockSpec((1, d), const),
                  pl.BlockSpec((d, d), const), pl.BlockSpec((pd, d), const), pl.BlockSpec((1, d), const)],
        out_specs=pl.BlockSpec((tm, d), row),
        out_shape=jax.ShapeDtypeStruct((n, d), F32),
        compiler_params=_params("parallel"),
        name="ple_final",
    )(x2, p2d, g_ple, w_gate, w_proj, g_final)


def _layer(x, p_i, layer_idx, g_mix, w_in, lambda_q1, lambda_k1, lambda_q2, lambda_k2, g_sb_out, g_df_out,
           w_out, g_ffn, w_router_group, b_router_group, w_router_expert, b_router_expert, w_expert_gate,
           w_expert_up, w_expert_down, g_ple, w_ple_gate, w_ple_proj, g_out):
    b, s, d = x.shape
    n = b * s
    tm = min(512, n)
    t = min(256, s)
    x2d = x.reshape(n, d)
    lambda_init = 0.8 - 0.6 * math.exp(-0.3 * layer_idx)

    proj = _inproj(x2d, g_mix.reshape(1, d), w_in.astype(BF16), tm)
    proj3 = proj.reshape(b, s, proj.shape[1])

    tri = (lax.broadcasted_iota(jnp.int32, (t, t), 0) > lax.broadcasted_iota(jnp.int32, (t, t), 1))
    u2 = jnp.concatenate([tri, tri], axis=0).astype(BF16)
    o_sb = _sb_attention(proj3, u2, t).reshape(n, SB_WIDTH)

    lam_rows = jnp.stack([lambda_q1, lambda_k1, lambda_q2, lambda_k2]).astype(F32)
    o_df = _df_attention(proj3, lam_rows, g_df_out.reshape(1, LANES).astype(F32), min(512, s),
                         lambda_init).reshape(n, DF_WIDTH)

    w_router = jnp.concatenate([w_router_group, w_router_expert], axis=1).astype(F32)
    w_router = jnp.pad(w_router, ((0, 0), (0, LANES - w_router.shape[1])))
    wr_hi, wr_lo = _split_bf16(w_router)
    b_router = jnp.pad(jnp.concatenate([b_router_group, b_router_expert]).astype(F32),
                       (0, LANES - N_GROUPS - N_EXPERTS)).reshape(1, LANES)
    w_out_bf = w_out.astype(BF16)
    x1, h2, gates = _outproj(o_sb, o_df, x2d, g_sb_out.reshape(1, SB_WIDTH), w_out_bf[:SB_WIDTH],
                             w_out_bf[SB_WIDTH:], g_ffn.reshape(1, d),
                             jnp.concatenate([wr_hi, wr_lo, wr_hi], axis=0), b_router, tm)

    x2 = _moe(h2, gates, x1, w_expert_gate.astype(BF16), w_expert_up.astype(BF16),
              w_expert_down.astype(BF16), tm)

    out = _ple(x2, p_i.reshape(n, p_i.shape[-1]), g_ple.reshape(1, d), w_ple_gate.astype(BF16),
               w_ple_proj.astype(BF16), g_out.reshape(1, d), tm)
    return out.reshape(b, s, d)


def kernel(x, p, g_mix, w_in, lambda_q1, lambda_k1, lambda_q2, lambda_k2, g_sb_out, g_df_out, w_out, g_ffn,
           w_router_group, b_router_group, w_router_expert, b_router_expert, w_expert_gate, w_expert_up,
           w_expert_down, g_ple, w_ple_gate, w_ple_proj, g_final):
    depth = p.shape[0]
    assert depth == 1, "the final norm is fused into the single layer's last kernel"
    return _layer(x, p[0], 0, g_mix[0], w_in[0], lambda_q1[0], lambda_k1[0], lambda_q2[0], lambda_k2[0],
                  g_sb_out[0], g_df_out[0], w_out[0], g_ffn[0], w_router_group[0], b_router_group[0],
                  w_router_expert[0], b_router_expert[0], w_expert_gate[0], w_expert_up[0],
                  w_expert_down[0], g_ple[0], w_ple_gate[0], w_ple_proj[0], g_final)
```

```python
import functools
import math

import jax
import jax.numpy as jnp
from jax import lax
from jax.experimental import pallas as pl
from jax.experimental.pallas import tpu as pltpu

F32 = jnp.float32
BF16 = jnp.bfloat16

HEAD_DIM = 64
LANES = 128
N_SB_PAIRS = 4
N_DF_HEADS = 4
SB_WIDTH = 512
DF_WIDTH = 512
SCALE = HEAD_DIM ** -0.5
NORM_EPS = 1e-6
N_GROUPS = 4
EXPERTS_PER_GROUP = 4
N_EXPERTS = 16
ROUTER_LANE0 = N_GROUPS
ALIBI_SLOPES = tuple(2.0 ** (-8.0 * (h + 1) / N_DF_HEADS) for h in range(N_DF_HEADS))
SB_LOG_ZERO = -110.0
VMEM_LIMIT = 48 * 1024 * 1024
MOE_TILE_ROWS = 512
COMBINE_ROWS = 256


def _rms(x, g):
    ms = jnp.mean(x * x, axis=-1, keepdims=True)
    return x * lax.rsqrt(ms + NORM_EPS) * g


def _dot(a, b):
    return jnp.dot(a, b, preferred_element_type=F32)


def _dot_nt(a, b):
    return lax.dot_general(a, b, (((1,), (1,)), ((), ())), preferred_element_type=F32)


def _params(*sem):
    return pltpu.CompilerParams(dimension_semantics=sem, vmem_limit_bytes=VMEM_LIMIT)


def _inproj_kernel(x_ref, g_ref, w_ref, o_ref, *, tn):
    h = _rms(x_ref[...], g_ref[...]).astype(BF16)
    for j in range(o_ref.shape[1] // tn):
        o_ref[:, j * tn:(j + 1) * tn] = _dot(h, w_ref[:, j * tn:(j + 1) * tn]).astype(o_ref.dtype)


def _inproj(x2d, g, w_bf16, tm):
    n, d = x2d.shape
    width = w_bf16.shape[1]
    return pl.pallas_call(
        functools.partial(_inproj_kernel, tn=1024),
        grid=(n // tm,),
        in_specs=[pl.BlockSpec((tm, d), lambda i: (i, 0)),
                  pl.BlockSpec((1, d), lambda i: (0, 0)),
                  pl.BlockSpec((d, width), lambda i: (0, 0))],
        out_specs=pl.BlockSpec((tm, width), lambda i: (i, 0)),
        out_shape=jax.ShapeDtypeStruct((n, width), BF16),
        compiler_params=_params("parallel"),
        name="inproj",
    )(x2d, g, w_bf16)


def _sb_kernel(q_ref, k_ref, v_ref, u_ref, o_ref, acc_ref, c_ref, *, t):
    qi = pl.program_id(2)
    lane = lax.broadcasted_iota(jnp.int32, (t, LANES), 1)
    first = lane < HEAD_DIM
    q = q_ref[0] * SCALE
    zero = jnp.zeros_like(q)
    q2 = jnp.concatenate([jnp.where(first, q, zero), jnp.where(first, zero, q)], axis=0)

    def block(kb, strict_mask):
        start = pl.multiple_of(kb * t, t)
        k = k_ref[0, pl.ds(start, t), :]
        v = v_ref[0, pl.ds(start, t), :]
        z = _dot_nt(q2, k)
        soft = jnp.log1p(jnp.exp(-jnp.abs(z)))
        log_beta = jnp.minimum(z, 0.0) - soft
        log_keep = -jnp.maximum(z, 0.0) - soft
        if strict_mask is not None:
            log_keep = jnp.where(strict_mask, log_keep, 0.0)
        hi = log_keep.astype(BF16)
        lo = (log_keep - hi.astype(F32)).astype(BF16)
        rev = _dot(jnp.concatenate([hi, lo], axis=1), u_ref[...])
        c = c_ref[...]
        w = jnp.exp(log_beta + rev + c)
        if strict_mask is not None:
            w = jnp.where(strict_mask, w, 0.0)
        w = w.astype(BF16)
        vz = jnp.zeros_like(v)
        v2 = jnp.concatenate([jnp.where(first, v, vz), jnp.where(first, vz, v)], axis=0)
        acc_ref[...] += _dot(jnp.concatenate([w[:t], w[t:]], axis=1), v2)
        c_new = c + jnp.sum(log_keep, axis=-1, keepdims=True)
        c_ref[...] = c_new
        return jnp.max(c_new)

    acc_ref[...] = jnp.zeros_like(acc_ref)
    c_ref[...] = jnp.zeros_like(c_ref)
    row = lax.broadcasted_iota(jnp.int32, (2 * t, t), 0)
    col = lax.broadcasted_iota(jnp.int32, (2 * t, t), 1)
    strict = col < jnp.where(row >= t, row - t, row)
    cmax = block(qi, strict)

    def cond(carry):
        kb, cm = carry
        return jnp.logical_and(kb >= 0, cm > SB_LOG_ZERO)

    def body(carry):
        kb, _ = carry
        return kb - 1, block(kb, None)

    lax.while_loop(cond, body, (qi - 1, cmax))
    o_ref[0] = acc_ref[...]


def _sb_attention(proj3, u2, t):
    b, s, _ = proj3.shape
    return pl.pallas_call(
        functools.partial(_sb_kernel, t=t),
        grid=(b, N_SB_PAIRS, s // t),
        in_specs=[pl.BlockSpec((1, t, LANES), lambda bi, hp, qi: (bi, qi, hp)),
                  pl.BlockSpec((1, s, LANES), lambda bi, hp, qi: (bi, 0, N_SB_PAIRS + hp)),
                  pl.BlockSpec((1, s, LANES), lambda bi, hp, qi: (bi, 0, 2 * N_SB_PAIRS + hp)),
                  pl.BlockSpec((2 * t, t), lambda bi, hp, qi: (0, 0))],
        out_specs=pl.BlockSpec((1, t, LANES), lambda bi, hp, qi: (bi, qi, hp)),
        out_shape=jax.ShapeDtypeStruct((b, s, SB_WIDTH), F32),
        scratch_shapes=[pltpu.VMEM((t, LANES), F32), pltpu.VMEM((2 * t, 1), F32)],
        compiler_params=_params("parallel", "parallel", "arbitrary"),
        name="sb_attention",
    )(proj3, proj3, proj3, u2)


def _df_kernel(q_ref, k_ref, v_ref, lam_ref, g_ref, o_ref, acc_ref, m_ref, *, t, lambda_init):
    h = pl.program_id(1)
    qi = pl.program_id(2)
    slope = jnp.float32(ALIBI_SLOPES[-1])
    for idx in range(N_DF_HEADS - 2, -1, -1):
        slope = jnp.where(h == idx, jnp.float32(ALIBI_SLOPES[idx]), slope)
    lane = lax.broadcasted_iota(jnp.int32, (t, LANES), 1)
    row = lax.broadcasted_iota(jnp.int32, (t, LANES), 0)
    first = lane < HEAD_DIM
    q = q_ref[0] * SCALE
    zero = jnp.zeros_like(q)
    bias_on = jnp.where(lane < 2, 1.0, 0.0).astype(BF16)
    q2 = jnp.concatenate([jnp.concatenate([jnp.where(first, q, zero), bias_on], axis=1),
                          jnp.concatenate([jnp.where(first, zero, q), bias_on], axis=1)], axis=0)
    key_lo = jnp.where(lane == 0, ((row >> 8) << 8).astype(F32),
                       jnp.where(lane == 1, (row & 255).astype(F32), 0.0)) * slope
    lane0 = lane == 0
    ones_v = jnp.ones((t, LANES), BF16)

    def block(kb, causal_mask):
        start = pl.multiple_of(kb * t, t)
        k = k_ref[0, pl.ds(start, t), :]
        v = v_ref[0, pl.ds(start, t), :]
        offset = slope * ((kb - qi) * t).astype(F32)
        k_bias = (key_lo + jnp.where(lane0, offset, 0.0)).astype(BF16)
        z = _dot_nt(q2, jnp.concatenate([k, k_bias], axis=1))
        if causal_mask is not None:
            z = jnp.where(causal_mask, z, -jnp.inf)
        m_old = m_ref[...]
        m_new = jnp.maximum(m_old, jnp.max(z, axis=-1, keepdims=True))
        alpha = jnp.exp(m_old - m_new)
        p = jnp.exp(z - jnp.concatenate([m_new] * (t // LANES), axis=1)).astype(BF16)
        pv = _dot(p, jnp.concatenate([v, ones_v], axis=1))
        acc_ref[...] = jnp.concatenate([alpha, alpha], axis=1) * acc_ref[...] + pv
        m_ref[...] = m_new

    acc_ref[...] = jnp.zeros_like(acc_ref)
    m_ref[...] = jnp.full_like(m_ref, -jnp.inf)
    r2 = lax.broadcasted_iota(jnp.int32, (2 * t, t), 0)
    c2 = lax.broadcasted_iota(jnp.int32, (2 * t, t), 1)
    block(qi, c2 <= jnp.where(r2 >= t, r2 - t, r2))

    def body(kb, carry):
        block(kb, None)
        return carry

    lax.fori_loop(0, qi, body, 0)

    lam_vec = lam_ref[...]
    lam = (jnp.exp(jnp.sum(lam_vec[0:1] * lam_vec[1:2], axis=-1, keepdims=True))
           - jnp.exp(jnp.sum(lam_vec[2:3] * lam_vec[3:4], axis=-1, keepdims=True))
           + lambda_init)
    acc = acc_ref[...]
    ratio = acc[:, :LANES] / acc[:, LANES:]
    o = ratio[:t] - lam * ratio[t:]
    o_ref[0] = (_rms(o, g_ref[...]) * (1.0 - lambda_init)).astype(o_ref.dtype)


def _df_attention(proj3, lam_rows, g_df, t, lambda_init):
    b, s, _ = proj3.shape
    col0 = 3 * N_SB_PAIRS
    return pl.pallas_call(
        functools.partial(_df_kernel, t=t, lambda_init=lambda_init),
        grid=(b, N_DF_HEADS, s // t),
        in_specs=[pl.BlockSpec((1, t, LANES), lambda bi, h, qi: (bi, qi, col0 + h)),
                  pl.BlockSpec((1, s, LANES), lambda bi, h, qi: (bi, 0, col0 + N_DF_HEADS + h)),
                  pl.BlockSpec((1, s, LANES), lambda bi, h, qi: (bi, 0, col0 + 2 * N_DF_HEADS + h)),
                  pl.BlockSpec((4, HEAD_DIM), lambda bi, h, qi: (0, 0)),
                  pl.BlockSpec((1, LANES), lambda bi, h, qi: (0, 0))],
        out_specs=pl.BlockSpec((1, t, LANES), lambda bi, h, qi: (bi, qi, h)),
        out_shape=jax.ShapeDtypeStruct((b, s, DF_WIDTH), BF16),
        scratch_shapes=[pltpu.VMEM((2 * t, 2 * LANES), F32), pltpu.VMEM((2 * t, LANES), F32)],
        compiler_params=_params("parallel", "parallel", "arbitrary"),
        name="df_attention",
    )(proj3, proj3, proj3, lam_rows, g_df)


def _split_bf16(x):
    hi = x.astype(BF16)
    return hi, (x - hi.astype(F32)).astype(BF16)


def _route(logits):
    lane = lax.broadcasted_iota(jnp.int32, logits.shape, 1)
    neg = -jnp.inf
    big = jnp.int32(LANES)
    gl = jnp.where(lane < N_GROUPS, logits, neg)
    gmax = jnp.max(gl, axis=-1, keepdims=True)
    g_idx = jnp.min(jnp.where(gl == gmax, lane, big), axis=-1, keepdims=True)
    g_w = 1.0 / jnp.sum(jnp.exp(gl - gmax), axis=-1, keepdims=True)
    e_lane = lane - ROUTER_LANE0
    in_group = jnp.logical_and(jnp.logical_and(e_lane >= 0, e_lane < N_EXPERTS),
                               (e_lane // EXPERTS_PER_GROUP) == g_idx)
    v1 = jnp.where(in_group, logits, neg)
    t1 = jnp.max(v1, axis=-1, keepdims=True)
    i1 = jnp.min(jnp.where(v1 == t1, lane, big), axis=-1, keepdims=True)
    v2 = jnp.where(lane == i1, neg, v1)
    t2 = jnp.max(v2, axis=-1, keepdims=True)
    i2 = jnp.min(jnp.where(v2 == t2, lane, big), axis=-1, keepdims=True)
    e2 = jnp.exp(t2 - t1)
    w1 = g_w / (1.0 + e2)
    w2 = w1 * e2
    id1 = (i1 - ROUTER_LANE0).astype(F32)
    id2 = (i2 - ROUTER_LANE0).astype(F32)
    return jnp.where(lane == 0, id1, jnp.where(lane == 1, id2,
                                               jnp.where(lane == 2, w1, jnp.where(lane == 3, w2, 0.0))))


def _outproj_kernel(osb_ref, odf_ref, x_ref, gsb_ref, wsb_ref, wdf_ref, gffn_ref, wr_ref, br_ref,
                    x1_ref, h2_ref, route_ref):
    a_sb = _rms(osb_ref[...], gsb_ref[...]).astype(BF16)
    x1 = x_ref[...] + _dot(a_sb, wsb_ref[...]) + _dot(odf_ref[...], wdf_ref[...])
    x1_ref[...] = x1
    h2 = _rms(x1, gffn_ref[...])
    h2_ref[...] = h2
    hi, lo = _split_bf16(h2)
    logits = _dot(jnp.concatenate([hi, hi, lo], axis=1), wr_ref[...]) + br_ref[...]
    route_ref[...] = _route(logits)


def _outproj(o_sb, o_df, x2d, g_sb, w_sb, w_df, g_ffn, w_router3, b_router, tm):
    n, d = x2d.shape
    row = lambda i: (i, 0)
    const = lambda i: (0, 0)
    return pl.pallas_call(
        _outproj_kernel,
        grid=(n // tm,),
        in_specs=[pl.BlockSpec((tm, SB_WIDTH), row), pl.BlockSpec((tm, DF_WIDTH), row),
                  pl.BlockSpec((tm, d), row), pl.BlockSpec((1, SB_WIDTH), const),
                  pl.BlockSpec((SB_WIDTH, d), const), pl.BlockSpec((DF_WIDTH, d), const),
                  pl.BlockSpec((1, d), const), pl.BlockSpec((3 * d, LANES), const),
                  pl.BlockSpec((1, LANES), const)],
        out_specs=[pl.BlockSpec((tm, d), row), pl.BlockSpec((tm, d), row),
                   pl.BlockSpec((tm, LANES), row)],
        out_shape=[jax.ShapeDtypeStruct((n, d), F32), jax.ShapeDtypeStruct((n, d), F32),
                   jax.ShapeDtypeStruct((n, LANES), F32)],
        compiler_params=_params("parallel"),
        name="outproj_router",
    )(o_sb, o_df, x2d, g_sb, w_sb, w_df, g_ffn, w_router3, b_router)


META_ROWS = 8


def _plan_kernel(route_ref, pos_ref, meta_ref, cnt_ref, run_ref, off_ref, *, tm, tile_rows):
    phase = pl.program_id(0)
    i = pl.program_id(1)
    route = route_ref[...]
    lane = lax.broadcasted_iota(jnp.int32, (tm, LANES), 1)
    lane_f = lane.astype(F32)
    sel1 = lane_f == route[:, 0:1]
    sel2 = lane_f == route[:, 1:2]
    onehot = jnp.where(jnp.logical_or(sel1, sel2), 1.0, 0.0)
    tile_count = jnp.sum(onehot, axis=0, keepdims=True)

    @pl.when(jnp.logical_and(phase == 0, i == 0))
    def _():
        cnt_ref[...] = jnp.zeros_like(cnt_ref)

    @pl.when(phase == 0)
    def _():
        cnt_ref[...] += tile_count

    @pl.when(jnp.logical_and(phase == 1, i == 0))
    def _():
        cnt = cnt_ref[...]
        n_tiles = jnp.ceil(cnt * (1.0 / tile_rows))
        r = lax.broadcasted_iota(jnp.int32, (LANES, LANES), 0)
        c = lax.broadcasted_iota(jnp.int32, (LANES, LANES), 1)
        tile_off = _dot(n_tiles.astype(BF16), jnp.where(r < c, 1.0, 0.0).astype(BF16))
        off_ref[...] = tile_off * tile_rows
        run_ref[...] = jnp.zeros_like(run_ref)
        mrow = lax.broadcasted_iota(jnp.int32, (META_ROWS, LANES), 0)
        meta_ref[...] = jnp.where(mrow == 0, tile_off, jnp.where(mrow == 1, n_tiles,
                                                                 jnp.where(mrow == 2, cnt, 0.0)))

    @pl.when(phase == 1)
    def _():
        r = lax.broadcasted_iota(jnp.int32, (tm, tm), 0)
        c = lax.broadcasted_iota(jnp.int32, (tm, tm), 1)
        earlier = _dot(jnp.where(c < r, 1.0, 0.0).astype(BF16), onehot.astype(BF16))
        base = earlier + run_ref[0:1] + off_ref[0:1]
        pos1 = jnp.sum(jnp.where(sel1, base, 0.0), axis=-1, keepdims=True)
        pos2 = jnp.sum(jnp.where(sel2, base, 0.0), axis=-1, keepdims=True)
        pos_ref[...] = jnp.where(lane == 0, pos1, jnp.where(lane == 1, pos2, 0.0)).astype(jnp.int32)
        run_ref[...] += tile_count


def _plan(route, tm, tile_rows):
    n = route.shape[0]
    assert n // tile_rows + 1 <= 256, "per-expert tile counts must stay exact in bf16"
    small = pltpu.VMEM((META_ROWS, LANES), F32)
    return pl.pallas_call(
        functools.partial(_plan_kernel, tm=tm, tile_rows=tile_rows),
        grid=(2, n // tm),
        in_specs=[pl.BlockSpec((tm, LANES), lambda ph, i: (i, 0))],
        out_specs=[pl.BlockSpec((tm, LANES), lambda ph, i: (i * ph, 0)),
                   pl.BlockSpec((META_ROWS, LANES), lambda ph, i: (0, 0))],
        out_shape=[jax.ShapeDtypeStruct((n, LANES), jnp.int32),
                   jax.ShapeDtypeStruct((META_ROWS, LANES), F32)],
        scratch_shapes=[small, small, small],
        compiler_params=_params("arbitrary", "arbitrary"),
        name="moe_plan",
    )(route)


def _row_copy(src_ref, src_row, dst_ref, dst_row, sem):
    return pltpu.make_async_copy(src_ref.at[pl.ds(src_row, 1)], dst_ref.at[pl.ds(dst_row, 1)], sem)


def _dispatch_kernel(meta_ref, h_ref, pos_ref, xs_ref, zero_ref, sem, *, tm, tile_rows):
    i = pl.program_id(0)

    def start(r, carry):
        for slot in range(2):
            _row_copy(h_ref, r, xs_ref, pos_ref[0, 0, 2 * r + slot], sem).start()
        return carry

    def wait(r, carry):
        _row_copy(h_ref, 0, xs_ref, 0, sem).wait()
        return carry

    lax.fori_loop(0, tm, start, 0, unroll=4)
    lax.fori_loop(0, 2 * tm, wait, 0, unroll=8)

    @pl.when(i == pl.num_programs(0) - 1)
    def _():
        zero_ref[...] = jnp.zeros_like(zero_ref)

        def per_expert(e, carry):
            off = meta_ref[0, e] * tile_rows
            lo = meta_ref[2, e]
            hi = meta_ref[1, e] * tile_rows

            def fill(r, c):
                _row_copy(zero_ref, 0, xs_ref, off + r, sem).start()
                return c

            def drain(r, c):
                _row_copy(zero_ref, 0, xs_ref, 0, sem).wait()
                return c

            lax.fori_loop(lo, hi, fill, 0)
            lax.fori_loop(lo, hi, drain, 0)
            return carry

        lax.fori_loop(0, N_EXPERTS, per_expert, 0)

        def spare_tile(j, carry):
            copy = pltpu.make_async_copy(zero_ref, xs_ref.at[pl.ds(j * tile_rows, tile_rows)], sem)
            copy.start()
            copy.wait()
            return carry

        n_used = meta_ref[0, N_EXPERTS - 1] + meta_ref[1, N_EXPERTS - 1]
        lax.fori_loop(n_used, xs_ref.shape[0] // tile_rows, spare_tile, 0)


def _dispatch(meta_i, h2, pos3, n_rows, tm, tile_rows):
    n, d = h2.shape
    return pl.pallas_call(
        functools.partial(_dispatch_kernel, tm=tm, tile_rows=tile_rows),
        grid_spec=pltpu.PrefetchScalarGridSpec(
            num_scalar_prefetch=1,
            grid=(n // tm,),
            in_specs=[pl.BlockSpec((tm, d), lambda i, meta: (i, 0)),
                      pl.BlockSpec((1, 1, 2 * tm), lambda i, meta: (i, 0, 0), memory_space=pltpu.SMEM)],
            out_specs=pl.BlockSpec(memory_space=pl.ANY),
            scratch_shapes=[pltpu.VMEM((tile_rows, d), F32), pltpu.SemaphoreType.DMA(())]),
        out_shape=jax.ShapeDtypeStruct((n_rows, d), F32),
        compiler_params=_params("arbitrary"),
        name="moe_dispatch",
    )(meta_i, h2, pos3)


def _gmm_kernel(te_ref, nv_ref, x_ref, wg_ref, wu_ref, wd_ref, y_ref):
    used = pl.program_id(0) < nv_ref[0]

    @pl.when(used)
    def _():
        x = x_ref[...].astype(BF16)
        a = _dot(x, wg_ref[0])
        hid = (a * jax.nn.sigmoid(a)) * _dot(x, wu_ref[0])
        y_ref[...] = _dot(hid.astype(BF16), wd_ref[0])

    @pl.when(jnp.logical_not(used))
    def _():
        y_ref[...] = jnp.zeros_like(y_ref)


def _gmm(tile_expert, n_valid, xs, wg, wu, wd, tile_rows):
    n_rows, d = xs.shape
    de = wg.shape[2]
    rows = lambda j, te, nv: (jnp.minimum(j, nv[0] - 1), 0)
    return pl.pallas_call(
        _gmm_kernel,
        grid_spec=pltpu.PrefetchScalarGridSpec(
            num_scalar_prefetch=2,
            grid=(n_rows // tile_rows,),
            in_specs=[pl.BlockSpec((tile_rows, d), rows),
                      pl.BlockSpec((1, d, de), lambda j, te, nv: (te[j], 0, 0)),
                      pl.BlockSpec((1, d, de), lambda j, te, nv: (te[j], 0, 0)),
                      pl.BlockSpec((1, de, d), lambda j, te, nv: (te[j], 0, 0))],
            out_specs=pl.BlockSpec((tile_rows, d), lambda j, te, nv: (j, 0))),
        out_shape=jax.ShapeDtypeStruct((n_rows, d), F32),
        compiler_params=_params("arbitrary"),
        name="moe_experts",
    )(tile_expert, n_valid, xs, wg, wu, wd)


def _ple_kernel(x1_ref, p_ref, route_ref, pos_ref, ys_ref, gple_ref, wgate_ref, wproj_ref, gfin_ref, o_ref,
                y_ref, sem, *, tm):
    def start(r, carry):
        for slot in range(2):
            _row_copy(ys_ref, pos_ref[0, 0, 2 * r + slot], y_ref.at[slot], r, sem).start()
        return carry

    def wait(r, carry):
        _row_copy(ys_ref, 0, y_ref.at[0], 0, sem).wait()
        return carry

    lax.fori_loop(0, tm, start, 0, unroll=4)
    emb = _dot(p_ref[...].astype(BF16), wproj_ref[...])
    lax.fori_loop(0, 2 * tm, wait, 0, unroll=8)
    route = route_ref[...]
    x = x1_ref[...] + (route[:, 2:3] * y_ref[0] + route[:, 3:4] * y_ref[1])
    gate = jax.nn.sigmoid(_dot(_rms(x, gple_ref[...]).astype(BF16), wgate_ref[...]))
    o_ref[...] = _rms(x + gate * emb, gfin_ref[...])


def _ple(x1, p2d, route, pos3, ys, g_ple, w_gate, w_proj, g_final, tm):
    n, d = x1.shape
    pd = p2d.shape[1]
    row = lambda i: (i, 0)
    const = lambda i: (0, 0)
    return pl.pallas_call(
        functools.partial(_ple_kernel, tm=tm),
        grid=(n // tm,),
        in_specs=[pl.BlockSpec((tm, d), row), pl.BlockSpec((tm, pd), row), pl.BlockSpec((tm, LANES), row),
                  pl.BlockSpec((1, 1, 2 * tm), lambda i: (i, 0, 0), memory_space=pltpu.SMEM),
                  pl.BlockSpec(memory_space=pl.ANY), pl.BlockSpec((1, d), const),
                  pl.BlockSpec((d, d), const), pl.BlockSpec((pd, d), const), pl.BlockSpec((1, d), const)],
        out_specs=pl.BlockSpec((tm, d), row),
        out_shape=jax.ShapeDtypeStruct((n, d), F32),
        scratch_shapes=[pltpu.VMEM((2, tm, d), F32), pltpu.SemaphoreType.DMA(())],
        compiler_params=_params("arbitrary"),
        name="combine_ple_final",
    )(x1, p2d, route, pos3, ys, g_ple, w_gate, w_proj, g_final)


def _layer(x, p_i, layer_idx, g_mix, w_in, lambda_q1, lambda_k1, lambda_q2, lambda_k2, g_sb_out, g_df_out,
           w_out, g_ffn, w_router_group, b_router_group, w_router_expert, b_router_expert, w_expert_gate,
           w_expert_up, w_expert_down, g_ple, w_ple_gate, w_ple_proj, g_out):
    b, s, d = x.shape
    n = b * s
    tm = min(512, n)
    t = min(256, s)
    x2d = x.reshape(n, d)
    lambda_init = 0.8 - 0.6 * math.exp(-0.3 * layer_idx)

    proj = _inproj(x2d, g_mix.reshape(1, d), w_in.astype(BF16), tm)
    proj3 = proj.reshape(b, s, proj.shape[1])

    tri = (lax.broadcasted_iota(jnp.int32, (t, t), 0) > lax.broadcasted_iota(jnp.int32, (t, t), 1))
    u2 = jnp.concatenate([tri, tri], axis=0).astype(BF16)
    o_sb = _sb_attention(proj3, u2, t).reshape(n, SB_WIDTH)

    lam_rows = jnp.stack([lambda_q1, lambda_k1, lambda_q2, lambda_k2]).astype(F32)
    o_df = _df_attention(proj3, lam_rows, g_df_out.reshape(1, LANES).astype(F32), min(512, s),
                         lambda_init).reshape(n, DF_WIDTH)

    w_router = jnp.concatenate([w_router_group, w_router_expert], axis=1).astype(F32)
    w_router = jnp.pad(w_router, ((0, 0), (0, LANES - w_router.shape[1])))
    wr_hi, wr_lo = _split_bf16(w_router)
    b_router = jnp.pad(jnp.concatenate([b_router_group, b_router_expert]).astype(F32),
                       (0, LANES - N_GROUPS - N_EXPERTS)).reshape(1, LANES)
    w_out_bf = w_out.astype(BF16)
    x1, h2, route = _outproj(o_sb, o_df, x2d, g_sb_out.reshape(1, SB_WIDTH), w_out_bf[:SB_WIDTH],
                             w_out_bf[SB_WIDTH:], g_ffn.reshape(1, d),
                             jnp.concatenate([wr_hi, wr_lo, wr_hi], axis=0), b_router, tm)

    tile_rows = min(MOE_TILE_ROWS, n)
    pos, meta = _plan(route, tm, tile_rows)
    n_tiles = 2 * n // tile_rows + N_EXPERTS
    meta_i = meta[:3, :N_EXPERTS].astype(jnp.int32)
    ends = meta_i[0] + meta_i[1]
    n_valid = ends[-1:]
    tile_ids = jnp.arange(n_tiles, dtype=jnp.int32)
    tile_expert = jnp.sum(tile_ids[:, None] >= ends[None, :], axis=1).astype(jnp.int32)
    tile_expert = jnp.where(tile_ids < n_valid, tile_expert, tile_expert[n_valid[0] - 1])
    pos2 = pos[:, :2]
    xs = _dispatch(meta_i, h2, pos2.reshape(n // tm, 1, 2 * tm), n_tiles * tile_rows, tm, tile_rows)
    ys = _gmm(tile_expert, n_valid, xs, w_expert_gate.astype(BF16), w_expert_up.astype(BF16),
              w_expert_down.astype(BF16), tile_rows)

    tc = min(COMBINE_ROWS, n)
    out = _ple(x1, p_i.reshape(n, p_i.shape[-1]), route, pos2.reshape(n // tc, 1, 2 * tc), ys,
               g_ple.reshape(1, d), w_ple_gate.astype(BF16), w_ple_proj.astype(BF16), g_out.reshape(1, d), tc)
    return out.reshape(b, s, d)


def kernel(x, p, g_mix, w_in, lambda_q1, lambda_k1, lambda_q2, lambda_k2, g_sb_out, g_df_out, w_out, g_ffn,
           w_router_group, b_router_group, w_router_expert, b_router_expert, w_expert_gate, w_expert_up,
           w_expert_down, g_ple, w_ple_gate, w_ple_proj, g_final):
    depth = p.shape[0]
    assert depth == 1, "the final norm is fused into the single layer's last kernel"
    return _layer(x, p[0], 0, g_mix[0], w_in[0], lambda_q1[0], lambda_k1[0], lambda_q2[0], lambda_k2[0],
                  g_sb_out[0], g_df_out[0], w_out[0], g_ffn[0], w_router_group[0], b_router_group[0],
                  w_router_expert[0], b_router_expert[0], w_expert_gate[0], w_expert_up[0],
                  w_expert_down[0], g_ple[0], w_ple_gate[0], w_ple_proj[0], g_final)
```

```python
import functools
import math

import jax
import jax.numpy as jnp
from jax import lax
from jax.experimental import pallas as pl
from jax.experimental.pallas import tpu as pltpu

F32 = jnp.float32
BF16 = jnp.bfloat16

HEAD_DIM = 64
LANES = 128
N_SB_PAIRS = 4
N_DF_HEADS = 4
SB_WIDTH = 512
DF_WIDTH = 512
SCALE = HEAD_DIM ** -0.5
NORM_EPS = 1e-6
N_GROUPS = 4
EXPERTS_PER_GROUP = 4
N_EXPERTS = 16
ROUTER_LANE0 = N_GROUPS
ALIBI_SLOPES = tuple(2.0 ** (-8.0 * (h + 1) / N_DF_HEADS) for h in range(N_DF_HEADS))
SB_LOG_ZERO = -110.0
VMEM_LIMIT = 48 * 1024 * 1024
MOE_TILE_ROWS = 512
COMBINE_ROWS = 256


def _rms(x, g):
    ms = jnp.mean(x * x, axis=-1, keepdims=True)
    return x * lax.rsqrt(ms + NORM_EPS) * g


def _dot(a, b):
    return jnp.dot(a, b, preferred_element_type=F32)


def _dot_nt(a, b):
    return lax.dot_general(a, b, (((1,), (1,)), ((), ())), preferred_element_type=F32)


def _rows_dot(dot, a, b, rows):
    return jnp.concatenate([dot(a[r:r + rows], b) for r in range(0, a.shape[0], rows)], axis=0)


def _params(*sem):
    return pltpu.CompilerParams(dimension_semantics=sem, vmem_limit_bytes=VMEM_LIMIT)


def _inproj_kernel(x_ref, g_ref, w_ref, o_ref, *, tn):
    h = _rms(x_ref[...], g_ref[...]).astype(BF16)
    for j in range(o_ref.shape[1] // tn):
        o_ref[:, j * tn:(j + 1) * tn] = _dot(h, w_ref[:, j * tn:(j + 1) * tn]).astype(o_ref.dtype)


def _inproj(x2d, g, w_bf16, tm):
    n, d = x2d.shape
    width = w_bf16.shape[1]
    return pl.pallas_call(
        functools.partial(_inproj_kernel, tn=1024),
        grid=(n // tm,),
        in_specs=[pl.BlockSpec((tm, d), lambda i: (i, 0)),
                  pl.BlockSpec((1, d), lambda i: (0, 0)),
                  pl.BlockSpec((d, width), lambda i: (0, 0))],
        out_specs=pl.BlockSpec((tm, width), lambda i: (i, 0)),
        out_shape=jax.ShapeDtypeStruct((n, width), BF16),
        compiler_params=_params("parallel"),
        name="inproj",
    )(x2d, g, w_bf16)


SB_STREAMS = 2


def _sb_kernel(q_ref, k_ref, v_ref, u_ref, o_ref, acc_ref, c_ref, *, t):
    qi = pl.program_id(2)
    lane = lax.broadcasted_iota(jnp.int32, (t, LANES), 1)
    first = lane < HEAD_DIM

    def stacked_q(p):
        q = q_ref[0, :, p * LANES:(p + 1) * LANES] * SCALE
        zero = jnp.zeros_like(q)
        return jnp.concatenate([jnp.where(first, q, zero), jnp.where(first, zero, q)], axis=0)

    q2 = [stacked_q(p) for p in range(SB_STREAMS)]

    def pair_block(p, kb, strict_mask):
        start = pl.multiple_of(kb * t, t)
        k = k_ref[0, pl.ds(start, t), p * LANES:(p + 1) * LANES]
        v = v_ref[0, pl.ds(start, t), p * LANES:(p + 1) * LANES]
        z = _rows_dot(_dot_nt, q2[p], k, t)
        soft = jnp.log(1.0 + jnp.exp(-jnp.abs(z)))
        log_beta = jnp.minimum(z, 0.0) - soft
        log_keep = log_beta - z
        if strict_mask is not None:
            log_keep = jnp.where(strict_mask, log_keep, 0.0)
        hi = log_keep.astype(BF16)
        lo = (log_keep - hi.astype(F32)).astype(BF16)
        rev = _rows_dot(_dot, jnp.concatenate([hi, lo], axis=1), u_ref[...], t)
        c = c_ref[p]
        w = jnp.exp(log_beta + rev + c)
        if strict_mask is not None:
            w = jnp.where(strict_mask, w, 0.0)
        w = w.astype(BF16)
        vz = jnp.zeros_like(v)
        v2 = jnp.concatenate([jnp.where(first, v, vz), jnp.where(first, vz, v)], axis=0)
        acc_ref[p] += _dot(jnp.concatenate([w[:t], w[t:]], axis=1), v2)
        c_new = c + jnp.sum(log_keep, axis=-1, keepdims=True)
        c_ref[p] = c_new
        return jnp.max(c_new)

    def block(kb, strict_mask):
        worst = pair_block(0, kb, strict_mask)
        for p in range(1, SB_STREAMS):
            worst = jnp.maximum(worst, pair_block(p, kb, strict_mask))
        return worst

    acc_ref[...] = jnp.zeros_like(acc_ref)
    c_ref[...] = jnp.zeros_like(c_ref)
    row = lax.broadcasted_iota(jnp.int32, (2 * t, t), 0)
    col = lax.broadcasted_iota(jnp.int32, (2 * t, t), 1)
    strict = col < jnp.where(row >= t, row - t, row)
    cmax = block(qi, strict)

    def cond(carry):
        kb, cm = carry
        return jnp.logical_and(kb >= 0, cm > SB_LOG_ZERO)

    def body(carry):
        kb, _ = carry
        return kb - 1, block(kb, None)

    lax.while_loop(cond, body, (qi - 1, cmax))
    for p in range(SB_STREAMS):
        o_ref[0, :, p * LANES:(p + 1) * LANES] = acc_ref[p]


def _sb_attention(proj3, u2, t):
    b, s, _ = proj3.shape
    groups = N_SB_PAIRS // SB_STREAMS
    width = SB_STREAMS * LANES
    return pl.pallas_call(
        functools.partial(_sb_kernel, t=t),
        grid=(b, groups, s // t),
        in_specs=[pl.BlockSpec((1, t, width), lambda bi, g, qi: (bi, qi, g)),
                  pl.BlockSpec((1, s, width), lambda bi, g, qi: (bi, 0, groups + g)),
                  pl.BlockSpec((1, s, width), lambda bi, g, qi: (bi, 0, 2 * groups + g)),
                  pl.BlockSpec((2 * t, t), lambda bi, g, qi: (0, 0))],
        out_specs=pl.BlockSpec((1, t, width), lambda bi, g, qi: (bi, qi, g)),
        out_shape=jax.ShapeDtypeStruct((b, s, SB_WIDTH), F32),
        scratch_shapes=[pltpu.VMEM((SB_STREAMS, t, LANES), F32), pltpu.VMEM((SB_STREAMS, 2 * t, 1), F32)],
        compiler_params=_params("parallel", "parallel", "arbitrary"),
        name="sb_attention",
    )(proj3, proj3, proj3, u2)


def _df_kernel(q_ref, k_ref, v_ref, lam_ref, g_ref, o_ref, acc_ref, m_ref, z_ref, *, t, lambda_init):
    h = pl.program_id(1)
    qi = pl.program_id(2)
    slope = jnp.float32(ALIBI_SLOPES[-1])
    for idx in range(N_DF_HEADS - 2, -1, -1):
        slope = jnp.where(h == idx, jnp.float32(ALIBI_SLOPES[idx]), slope)
    lane = lax.broadcasted_iota(jnp.int32, (t, LANES), 1)
    row = lax.broadcasted_iota(jnp.int32, (t, LANES), 0)
    first = lane < HEAD_DIM
    q = q_ref[0] * SCALE
    zero = jnp.zeros_like(q)
    bias_on = jnp.where(lane < 2, 1.0, 0.0).astype(BF16)
    q2 = jnp.concatenate([jnp.concatenate([jnp.where(first, q, zero), bias_on], axis=1),
                          jnp.concatenate([jnp.where(first, zero, q), bias_on], axis=1)], axis=0)
    key_lo = jnp.where(lane == 0, ((row >> 8) << 8).astype(F32),
                       jnp.where(lane == 1, (row & 255).astype(F32), 0.0)) * slope
    lane0 = lane == 0
    ones_v = jnp.ones((t, LANES), BF16)

    def logits(kb):
        k = k_ref[0, pl.ds(pl.multiple_of(kb * t, t), t), :]
        offset = slope * ((kb - qi) * t).astype(F32)
        k_bias = (key_lo + jnp.where(lane0, offset, 0.0)).astype(BF16)
        return _dot_nt(q2, jnp.concatenate([k, k_bias], axis=1))

    def accumulate(z, kb):
        v = v_ref[0, pl.ds(pl.multiple_of(kb * t, t), t), :]
        m_old = m_ref[...]
        m_new = jnp.maximum(m_old, jnp.max(z, axis=-1, keepdims=True))
        alpha = jnp.exp(m_old - m_new)
        p = jnp.exp(z - jnp.concatenate([m_new] * (t // LANES), axis=1)).astype(BF16)
        pv = _dot(p, jnp.concatenate([v, ones_v], axis=1))
        acc_ref[...] = jnp.concatenate([alpha, alpha], axis=1) * acc_ref[...] + pv
        m_ref[...] = m_new

    acc_ref[...] = jnp.zeros_like(acc_ref)
    m_ref[...] = jnp.full_like(m_ref, -jnp.inf)
    r2 = lax.broadcasted_iota(jnp.int32, (2 * t, t), 0)
    c2 = lax.broadcasted_iota(jnp.int32, (2 * t, t), 1)
    z_ref[...] = jnp.where(c2 <= jnp.where(r2 >= t, r2 - t, r2), logits(qi), -jnp.inf)

    def body(j, carry):
        z = z_ref[...]
        z_ref[...] = logits(j)
        accumulate(z, jnp.where(j == 0, qi, j - 1))
        return carry

    lax.fori_loop(0, qi, body, 0)
    accumulate(z_ref[...], jnp.where(qi == 0, qi, qi - 1))

    lam_vec = lam_ref[...]
    lam = (jnp.exp(jnp.sum(lam_vec[0:1] * lam_vec[1:2], axis=-1, keepdims=True))
           - jnp.exp(jnp.sum(lam_vec[2:3] * lam_vec[3:4], axis=-1, keepdims=True))
           + lambda_init)
    acc = acc_ref[...]
    ratio = acc[:, :LANES] / acc[:, LANES:]
    o = ratio[:t] - lam * ratio[t:]
    o_ref[0] = (_rms(o, g_ref[...]) * (1.0 - lambda_init)).astype(o_ref.dtype)


def _df_attention(proj3, lam_rows, g_df, t, lambda_init):
    b, s, _ = proj3.shape
    col0 = 3 * N_SB_PAIRS
    return pl.pallas_call(
        functools.partial(_df_kernel, t=t, lambda_init=lambda_init),
        grid=(b, N_DF_HEADS, s // t),
        in_specs=[pl.BlockSpec((1, t, LANES), lambda bi, h, qi: (bi, qi, col0 + h)),
                  pl.BlockSpec((1, s, LANES), lambda bi, h, qi: (bi, 0, col0 + N_DF_HEADS + h)),
                  pl.BlockSpec((1, s, LANES), lambda bi, h, qi: (bi, 0, col0 + 2 * N_DF_HEADS + h)),
                  pl.BlockSpec((4, HEAD_DIM), lambda bi, h, qi: (0, 0)),
                  pl.BlockSpec((1, LANES), lambda bi, h, qi: (0, 0))],
        out_specs=pl.BlockSpec((1, t, LANES), lambda bi, h, qi: (bi, qi, h)),
        out_shape=jax.ShapeDtypeStruct((b, s, DF_WIDTH), BF16),
        scratch_shapes=[pltpu.VMEM((2 * t, 2 * LANES), F32), pltpu.VMEM((2 * t, LANES), F32),
                        pltpu.VMEM((2 * t, t), F32)],
        compiler_params=_params("parallel", "parallel", "arbitrary"),
        name="df_attention",
    )(proj3, proj3, proj3, lam_rows, g_df)


def _split_bf16(x):
    hi = x.astype(BF16)
    return hi, (x - hi.astype(F32)).astype(BF16)


def _route(logits):
    lane = lax.broadcasted_iota(jnp.int32, logits.shape, 1)
    neg = -jnp.inf
    big = jnp.int32(LANES)
    gl = jnp.where(lane < N_GROUPS, logits, neg)
    gmax = jnp.max(gl, axis=-1, keepdims=True)
    g_idx = jnp.min(jnp.where(gl == gmax, lane, big), axis=-1, keepdims=True)
    g_w = 1.0 / jnp.sum(jnp.exp(gl - gmax), axis=-1, keepdims=True)
    e_lane = lane - ROUTER_LANE0
    in_group = jnp.logical_and(jnp.logical_and(e_lane >= 0, e_lane < N_EXPERTS),
                               (e_lane // EXPERTS_PER_GROUP) == g_idx)
    v1 = jnp.where(in_group, logits, neg)
    t1 = jnp.max(v1, axis=-1, keepdims=True)
    i1 = jnp.min(jnp.where(v1 == t1, lane, big), axis=-1, keepdims=True)
    v2 = jnp.where(lane == i1, neg, v1)
    t2 = jnp.max(v2, axis=-1, keepdims=True)
    i2 = jnp.min(jnp.where(v2 == t2, lane, big), axis=-1, keepdims=True)
    e2 = jnp.exp(t2 - t1)
    w1 = g_w / (1.0 + e2)
    w2 = w1 * e2
    id1 = (i1 - ROUTER_LANE0).astype(F32)
    id2 = (i2 - ROUTER_LANE0).astype(F32)
    return jnp.where(lane == 0, id1, jnp.where(lane == 1, id2,
                                               jnp.where(lane == 2, w1, jnp.where(lane == 3, w2, 0.0))))


def _outproj_kernel(osb_ref, odf_ref, x_ref, gsb_ref, wsb_ref, wdf_ref, gffn_ref, wr_ref, br_ref,
                    x1_ref, h2_ref, route_ref):
    a_sb = _rms(osb_ref[...], gsb_ref[...]).astype(BF16)
    x1 = x_ref[...] + _dot(a_sb, wsb_ref[...]) + _dot(odf_ref[...], wdf_ref[...])
    x1_ref[...] = x1
    h2 = _rms(x1, gffn_ref[...])
    h2_ref[...] = h2
    hi, lo = _split_bf16(h2)
    logits = _dot(jnp.concatenate([hi, hi, lo], axis=1), wr_ref[...]) + br_ref[...]
    route_ref[...] = _route(logits)


def _outproj(o_sb, o_df, x2d, g_sb, w_sb, w_df, g_ffn, w_router3, b_router, tm):
    n, d = x2d.shape
    row = lambda i: (i, 0)
    const = lambda i: (0, 0)
    return pl.pallas_call(
        _outproj_kernel,
        grid=(n // tm,),
        in_specs=[pl.BlockSpec((tm, SB_WIDTH), row), pl.BlockSpec((tm, DF_WIDTH), row),
                  pl.BlockSpec((tm, d), row), pl.BlockSpec((1, SB_WIDTH), const),
                  pl.BlockSpec((SB_WIDTH, d), const), pl.BlockSpec((DF_WIDTH, d), const),
                  pl.BlockSpec((1, d), const), pl.BlockSpec((3 * d, LANES), const),
                  pl.BlockSpec((1, LANES), const)],
        out_specs=[pl.BlockSpec((tm, d), row), pl.BlockSpec((tm, d), row),
                   pl.BlockSpec((tm, LANES), row)],
        out_shape=[jax.ShapeDtypeStruct((n, d), F32), jax.ShapeDtypeStruct((n, d), F32),
                   jax.ShapeDtypeStruct((n, LANES), F32)],
        compiler_params=_params("parallel"),
        name="outproj_router",
    )(o_sb, o_df, x2d, g_sb, w_sb, w_df, g_ffn, w_router3, b_router)


META_ROWS = 8


def _plan_kernel(route_ref, pos_ref, meta_ref, cnt_ref, run_ref, off_ref, *, tm, tile_rows):
    phase = pl.program_id(0)
    i = pl.program_id(1)
    route = route_ref[...]
    lane = lax.broadcasted_iota(jnp.int32, (tm, LANES), 1)
    lane_f = lane.astype(F32)
    sel1 = lane_f == route[:, 0:1]
    sel2 = lane_f == route[:, 1:2]
    onehot = jnp.where(jnp.logical_or(sel1, sel2), 1.0, 0.0)
    tile_count = jnp.sum(onehot, axis=0, keepdims=True)

    @pl.when(jnp.logical_and(phase == 0, i == 0))
    def _():
        cnt_ref[...] = jnp.zeros_like(cnt_ref)

    @pl.when(phase == 0)
    def _():
        cnt_ref[...] += tile_count

    @pl.when(jnp.logical_and(phase == 1, i == 0))
    def _():
        cnt = cnt_ref[...]
        n_tiles = jnp.ceil(cnt * (1.0 / tile_rows))
        r = lax.broadcasted_iota(jnp.int32, (LANES, LANES), 0)
        c = lax.broadcasted_iota(jnp.int32, (LANES, LANES), 1)
        tile_off = _dot(n_tiles.astype(BF16), jnp.where(r < c, 1.0, 0.0).astype(BF16))
        off_ref[...] = tile_off * tile_rows
        run_ref[...] = jnp.zeros_like(run_ref)
        mrow = lax.broadcasted_iota(jnp.int32, (META_ROWS, LANES), 0)
        meta_ref[...] = jnp.where(mrow == 0, tile_off, jnp.where(mrow == 1, n_tiles,
                                                                 jnp.where(mrow == 2, cnt, 0.0)))

    @pl.when(phase == 1)
    def _():
        r = lax.broadcasted_iota(jnp.int32, (tm, tm), 0)
        c = lax.broadcasted_iota(jnp.int32, (tm, tm), 1)
        earlier = _dot(jnp.where(c < r, 1.0, 0.0).astype(BF16), onehot.astype(BF16))
        base = earlier + run_ref[0:1] + off_ref[0:1]
        pos1 = jnp.sum(jnp.where(sel1, base, 0.0), axis=-1, keepdims=True)
        pos2 = jnp.sum(jnp.where(sel2, base, 0.0), axis=-1, keepdims=True)
        pos_ref[...] = jnp.where(lane == 0, pos1, jnp.where(lane == 1, pos2, 0.0)).astype(jnp.int32)
        run_ref[...] += tile_count


def _plan(route, tm, tile_rows):
    n = route.shape[0]
    assert n // tile_rows + 1 <= 256, "per-expert tile counts must stay exact in bf16"
    small = pltpu.VMEM((META_ROWS, LANES), F32)
    return pl.pallas_call(
        functools.partial(_plan_kernel, tm=tm, tile_rows=tile_rows),
        grid=(2, n // tm),
        in_specs=[pl.BlockSpec((tm, LANES), lambda ph, i: (i, 0))],
        out_specs=[pl.BlockSpec((tm, LANES), lambda ph, i: (i * ph, 0)),
                   pl.BlockSpec((META_ROWS, LANES), lambda ph, i: (0, 0))],
        out_shape=[jax.ShapeDtypeStruct((n, LANES), jnp.int32),
                   jax.ShapeDtypeStruct((META_ROWS, LANES), F32)],
        scratch_shapes=[small, small, small],
        compiler_params=_params("arbitrary", "arbitrary"),
        name="moe_plan",
    )(route)


def _row_copy(src_ref, src_row, dst_ref, dst_row, sem):
    return pltpu.make_async_copy(src_ref.at[pl.ds(src_row, 1)], dst_ref.at[pl.ds(dst_row, 1)], sem)


def _dispatch_kernel(meta_ref, h_ref, pos_ref, xs_ref, zero_ref, sem, *, tm, tile_rows):
    i = pl.program_id(0)

    def start(r, carry):
        for slot in range(2):
            _row_copy(h_ref, r, xs_ref, pos_ref[0, 0, 2 * r + slot], sem).start(priority=slot)
        return carry

    def wait(r, carry):
        _row_copy(h_ref, 0, xs_ref, 0, sem).wait()
        return carry

    lax.fori_loop(0, tm, start, 0, unroll=4)
    lax.fori_loop(0, 2 * tm, wait, 0, unroll=8)

    @pl.when(i == pl.num_programs(0) - 1)
    def _():
        zero_ref[...] = jnp.zeros_like(zero_ref)

        def per_expert(e, carry):
            off = meta_ref[0, e] * tile_rows
            lo = meta_ref[2, e]
            hi = meta_ref[1, e] * tile_rows

            def fill(r, c):
                _row_copy(zero_ref, 0, xs_ref, off + r, sem).start()
                return c

            def drain(r, c):
                _row_copy(zero_ref, 0, xs_ref, 0, sem).wait()
                return c

            lax.fori_loop(lo, hi, fill, 0)
            lax.fori_loop(lo, hi, drain, 0)
            return carry

        lax.fori_loop(0, N_EXPERTS, per_expert, 0)

        def spare_tile(j, carry):
            copy = pltpu.make_async_copy(zero_ref, xs_ref.at[pl.ds(j * tile_rows, tile_rows)], sem)
            copy.start()
            copy.wait()
            return carry

        n_used = meta_ref[0, N_EXPERTS - 1] + meta_ref[1, N_EXPERTS - 1]
        lax.fori_loop(n_used, xs_ref.shape[0] // tile_rows, spare_tile, 0)


def _dispatch(meta_i, h2, pos3, n_rows, tm, tile_rows):
    n, d = h2.shape
    return pl.pallas_call(
        functools.partial(_dispatch_kernel, tm=tm, tile_rows=tile_rows),
        grid_spec=pltpu.PrefetchScalarGridSpec(
            num_scalar_prefetch=1,
            grid=(n // tm,),
            in_specs=[pl.BlockSpec((tm, d), lambda i, meta: (i, 0)),
                      pl.BlockSpec((1, 1, 2 * tm), lambda i, meta: (i, 0, 0), memory_space=pltpu.SMEM)],
            out_specs=pl.BlockSpec(memory_space=pl.ANY),
            scratch_shapes=[pltpu.VMEM((tile_rows, d), F32), pltpu.SemaphoreType.DMA(())]),
        out_shape=jax.ShapeDtypeStruct((n_rows, d), F32),
        compiler_params=_params("arbitrary"),
        name="moe_dispatch",
    )(meta_i, h2, pos3)


def _gmm_kernel(te_ref, nv_ref, x_ref, wg_ref, wu_ref, wd_ref, y_ref):
    used = pl.program_id(0) < nv_ref[0]

    @pl.when(used)
    def _():
        x = x_ref[...].astype(BF16)
        a = _dot(x, wg_ref[0])
        hid = (a * jax.nn.sigmoid(a)) * _dot(x, wu_ref[0])
        y_ref[...] = _dot(hid.astype(BF16), wd_ref[0])

    @pl.when(jnp.logical_not(used))
    def _():
        y_ref[...] = jnp.zeros_like(y_ref)


def _gmm(tile_expert, n_valid, xs, wg, wu, wd, tile_rows):
    n_rows, d = xs.shape
    de = wg.shape[2]
    rows = lambda j, te, nv: (jnp.minimum(j, nv[0] - 1), 0)
    return pl.pallas_call(
        _gmm_kernel,
        grid_spec=pltpu.PrefetchScalarGridSpec(
            num_scalar_prefetch=2,
            grid=(n_rows // tile_rows,),
            in_specs=[pl.BlockSpec((tile_rows, d), rows),
                      pl.BlockSpec((1, d, de), lambda j, te, nv: (te[j], 0, 0)),
                      pl.BlockSpec((1, d, de), lambda j, te, nv: (te[j], 0, 0)),
                      pl.BlockSpec((1, de, d), lambda j, te, nv: (te[j], 0, 0))],
            out_specs=pl.BlockSpec((tile_rows, d), lambda j, te, nv: (j, 0))),
        out_shape=jax.ShapeDtypeStruct((n_rows, d), F32),
        compiler_params=_params("arbitrary"),
        name="moe_experts",
    )(tile_expert, n_valid, xs, wg, wu, wd)


def _ple_kernel(x1_ref, p_ref, route_ref, pos_ref, ys_ref, gple_ref, wgate_ref, wproj_ref, gfin_ref, o_ref,
                y_ref, sem, *, tm):
    def start(r, carry):
        for slot in range(2):
            _row_copy(ys_ref, pos_ref[0, 0, 2 * r + slot], y_ref.at[slot], r, sem).start(priority=slot)
        return carry

    def wait(r, carry):
        _row_copy(ys_ref, 0, y_ref.at[0], 0, sem).wait()
        return carry

    lax.fori_loop(0, tm, start, 0, unroll=4)
    emb = _dot(p_ref[...].astype(BF16), wproj_ref[...])
    lax.fori_loop(0, 2 * tm, wait, 0, unroll=8)
    route = route_ref[...]
    x = x1_ref[...] + (route[:, 2:3] * y_ref[0] + route[:, 3:4] * y_ref[1])
    gate = jax.nn.sigmoid(_dot(_rms(x, gple_ref[...]).astype(BF16), wgate_ref[...]))
    o_ref[...] = _rms(x + gate * emb, gfin_ref[...])


def _ple(x1, p2d, route, pos3, ys, g_ple, w_gate, w_proj, g_final, tm):
    n, d = x1.shape
    pd = p2d.shape[1]
    row = lambda i: (i, 0)
    const = lambda i: (0, 0)
    return pl.pallas_call(
        functools.partial(_ple_kernel, tm=tm),
        grid=(n // tm,),
        in_specs=[pl.BlockSpec((tm, d), row), pl.BlockSpec((tm, pd), row), pl.BlockSpec((tm, LANES), row),
                  pl.BlockSpec((1, 1, 2 * tm), lambda i: (i, 0, 0), memory_space=pltpu.SMEM),
                  pl.BlockSpec(memory_space=pl.ANY), pl.BlockSpec((1, d), const),
                  pl.BlockSpec((d, d), const), pl.BlockSpec((pd, d), const), pl.BlockSpec((1, d), const)],
        out_specs=pl.BlockSpec((tm, d), row),
        out_shape=jax.ShapeDtypeStruct((n, d), F32),
        scratch_shapes=[pltpu.VMEM((2, tm, d), F32), pltpu.SemaphoreType.DMA(())],
        compiler_params=_params("arbitrary"),
        name="combine_ple_final",
    )(x1, p2d, route, pos3, ys, g_ple, w_gate, w_proj, g_final)


def _layer(x, p_i, layer_idx, g_mix, w_in, lambda_q1, lambda_k1, lambda_q2, lambda_k2, g_sb_out, g_df_out,
           w_out, g_ffn, w_router_group, b_router_group, w_router_expert, b_router_expert, w_expert_gate,
           w_expert_up, w_expert_down, g_ple, w_ple_gate, w_ple_proj, g_out):
    b, s, d = x.shape
    n = b * s
    tm = min(512, n)
    t = min(256, s)
    x2d = x.reshape(n, d)
    lambda_init = 0.8 - 0.6 * math.exp(-0.3 * layer_idx)

    proj = _inproj(x2d, g_mix.reshape(1, d), w_in.astype(BF16), tm)
    proj3 = proj.reshape(b, s, proj.shape[1])

    tri = (lax.broadcasted_iota(jnp.int32, (t, t), 0) > lax.broadcasted_iota(jnp.int32, (t, t), 1))
    u2 = jnp.concatenate([tri, tri], axis=0).astype(BF16)
    o_sb = _sb_attention(proj3, u2, t).reshape(n, SB_WIDTH)

    lam_rows = jnp.stack([lambda_q1, lambda_k1, lambda_q2, lambda_k2]).astype(F32)
    o_df = _df_attention(proj3, lam_rows, g_df_out.reshape(1, LANES).astype(F32), min(512, s),
                         lambda_init).reshape(n, DF_WIDTH)

    w_router = jnp.concatenate([w_router_group, w_router_expert], axis=1).astype(F32)
    w_router = jnp.pad(w_router, ((0, 0), (0, LANES - w_router.shape[1])))
    wr_hi, wr_lo = _split_bf16(w_router)
    b_router = jnp.pad(jnp.concatenate([b_router_group, b_router_expert]).astype(F32),
                       (0, LANES - N_GROUPS - N_EXPERTS)).reshape(1, LANES)
    w_out_bf = w_out.astype(BF16)
    x1, h2, route = _outproj(o_sb, o_df, x2d, g_sb_out.reshape(1, SB_WIDTH), w_out_bf[:SB_WIDTH],
                             w_out_bf[SB_WIDTH:], g_ffn.reshape(1, d),
                             jnp.concatenate([wr_hi, wr_lo, wr_hi], axis=0), b_router, tm)

    tile_rows = min(MOE_TILE_ROWS, n)
    pos, meta = _plan(route, tm, tile_rows)
    n_tiles = 2 * n // tile_rows + N_EXPERTS
    meta_i = meta[:3, :N_EXPERTS].astype(jnp.int32)
    ends = meta_i[0] + meta_i[1]
    n_valid = ends[-1:]
    tile_ids = jnp.arange(n_tiles, dtype=jnp.int32)
    tile_expert = jnp.sum(tile_ids[:, None] >= ends[None, :], axis=1).astype(jnp.int32)
    tile_expert = jnp.where(tile_ids < n_valid, tile_expert, tile_expert[n_valid[0] - 1])
    pos2 = pos[:, :2]
    xs = _dispatch(meta_i, h2, pos2.reshape(n // tm, 1, 2 * tm), n_tiles * tile_rows, tm, tile_rows)
    ys = _gmm(tile_expert, n_valid, xs, w_expert_gate.astype(BF16), w_expert_up.astype(BF16),
              w_expert_down.astype(BF16), tile_rows)

    tc = min(COMBINE_ROWS, n)
    out = _ple(x1, p_i.reshape(n, p_i.shape[-1]), route, pos2.reshape(n // tc, 1, 2 * tc), ys,
               g_ple.reshape(1, d), w_ple_gate.astype(BF16), w_ple_proj.astype(BF16), g_out.reshape(1, d), tc)
    return out.reshape(b, s, d)


def kernel(x, p, g_mix, w_in, lambda_q1, lambda_k1, lambda_q2, lambda_k2, g_sb_out, g_df_out, w_out, g_ffn,
           w_router_group, b_router_group, w_router_expert, b_router_expert, w_expert_gate, w_expert_up,
           w_expert_down, g_ple, w_ple_gate, w_ple_proj, g_final):
    depth = p.shape[0]
    assert depth == 1, "the final norm is fused into the single layer's last kernel"
    return _layer(x, p[0], 0, g_mix[0], w_in[0], lambda_q1[0], lambda_k1[0], lambda_q2[0], lambda_k2[0],
                  g_sb_out[0], g_df_out[0], w_out[0], g_ffn[0], w_router_group[0], b_router_group[0],
                  w_router_expert[0], b_router_expert[0], w_expert_gate[0], w_expert_up[0],
                  w_expert_down[0], g_ple[0], w_ple_gate[0], w_ple_proj[0], g_final)
```

```python
import functools
import math

import jax
import jax.numpy as jnp
from jax import lax
from jax.experimental import pallas as pl
from jax.experimental.pallas import tpu as pltpu

F32 = jnp.float32
BF16 = jnp.bfloat16

HEAD_DIM = 64
LANES = 128
N_SB_PAIRS = 4
N_DF_HEADS = 4
SB_WIDTH = 512
DF_WIDTH = 512
SCALE = HEAD_DIM ** -0.5
NORM_EPS = 1e-6
N_GROUPS = 4
EXPERTS_PER_GROUP = 4
N_EXPERTS = 16
ROUTER_LANE0 = N_GROUPS
ALIBI_SLOPES = tuple(2.0 ** (-8.0 * (h + 1) / N_DF_HEADS) for h in range(N_DF_HEADS))
SB_LOG_ZERO = -110.0
VMEM_LIMIT = 48 * 1024 * 1024
MOE_TILE_ROWS = 512
COMBINE_ROWS = 256


def _rms(x, g):
    ms = jnp.mean(x * x, axis=-1, keepdims=True)
    return x * lax.rsqrt(ms + NORM_EPS) * g


def _dot(a, b):
    return jnp.dot(a, b, preferred_element_type=F32)


def _dot_nt(a, b):
    return lax.dot_general(a, b, (((1,), (1,)), ((), ())), preferred_element_type=F32)


def _rows_dot(dot, a, b, rows):
    return jnp.concatenate([dot(a[r:r + rows], b) for r in range(0, a.shape[0], rows)], axis=0)


def _params(*sem):
    return pltpu.CompilerParams(dimension_semantics=sem, vmem_limit_bytes=VMEM_LIMIT)


def _inproj_kernel(x_ref, g_ref, w_ref, o_ref, *, tn):
    h = _rms(x_ref[...], g_ref[...]).astype(BF16)
    for j in range(o_ref.shape[1] // tn):
        o_ref[:, j * tn:(j + 1) * tn] = _dot(h, w_ref[:, j * tn:(j + 1) * tn]).astype(o_ref.dtype)


def _inproj(x2d, g, w_bf16, tm):
    n, d = x2d.shape
    width = w_bf16.shape[1]
    return pl.pallas_call(
        functools.partial(_inproj_kernel, tn=1024),
        grid=(n // tm,),
        in_specs=[pl.BlockSpec((tm, d), lambda i: (i, 0)),
                  pl.BlockSpec((1, d), lambda i: (0, 0)),
                  pl.BlockSpec((d, width), lambda i: (0, 0))],
        out_specs=pl.BlockSpec((tm, width), lambda i: (i, 0)),
        out_shape=jax.ShapeDtypeStruct((n, width), BF16),
        compiler_params=_params("parallel"),
        name="inproj",
    )(x2d, g, w_bf16)


SB_STREAMS = 2


def _sb_kernel(q_ref, k_ref, v_ref, u_ref, o_ref, acc_ref, c_ref, *, t):
    qi = pl.program_id(2)
    lane = lax.broadcasted_iota(jnp.int32, (t, LANES), 1)
    first = lane < HEAD_DIM

    def stacked_q(p):
        q = q_ref[0, :, p * LANES:(p + 1) * LANES] * SCALE
        zero = jnp.zeros_like(q)
        return jnp.concatenate([jnp.where(first, q, zero), jnp.where(first, zero, q)], axis=0)

    q2 = [stacked_q(p) for p in range(SB_STREAMS)]

    def pair_block(p, kb, strict_mask):
        start = pl.multiple_of(kb * t, t)
        k = k_ref[0, pl.ds(start, t), p * LANES:(p + 1) * LANES]
        v = v_ref[0, pl.ds(start, t), p * LANES:(p + 1) * LANES]
        z = _rows_dot(_dot_nt, q2[p], k, t)
        soft = jnp.log(1.0 + jnp.exp(-jnp.abs(z)))
        log_beta = jnp.minimum(z, 0.0) - soft
        log_keep = log_beta - z
        if strict_mask is not None:
            log_keep = jnp.where(strict_mask, log_keep, 0.0)
        hi = log_keep.astype(BF16)
        lo = (log_keep - hi.astype(F32)).astype(BF16)
        rev = _rows_dot(_dot, jnp.concatenate([hi, lo], axis=1), u_ref[...], t)
        c = c_ref[p]
        w = jnp.exp(log_beta + rev + jnp.concatenate([c] * (t // LANES), axis=1))
        if strict_mask is not None:
            w = jnp.where(strict_mask, w, 0.0)
        w = w.astype(BF16)
        vz = jnp.zeros_like(v)
        v2 = jnp.concatenate([jnp.where(first, v, vz), jnp.where(first, vz, v)], axis=0)
        acc_ref[p] += _dot(jnp.concatenate([w[:t], w[t:]], axis=1), v2)
        c_new = c + jnp.sum(log_keep, axis=-1, keepdims=True)
        c_ref[p] = c_new
        return jnp.max(c_new)

    def block(kb, strict_mask):
        worst = pair_block(0, kb, strict_mask)
        for p in range(1, SB_STREAMS):
            worst = jnp.maximum(worst, pair_block(p, kb, strict_mask))
        return worst

    acc_ref[...] = jnp.zeros_like(acc_ref)
    c_ref[...] = jnp.zeros_like(c_ref)
    row = lax.broadcasted_iota(jnp.int32, (2 * t, t), 0)
    col = lax.broadcasted_iota(jnp.int32, (2 * t, t), 1)
    strict = col < jnp.where(row >= t, row - t, row)
    cmax = block(qi, strict)

    def cond(carry):
        kb, cm = carry
        return jnp.logical_and(kb >= 0, cm > SB_LOG_ZERO)

    def body(carry):
        kb, _ = carry
        return kb - 1, block(kb, None)

    lax.while_loop(cond, body, (qi - 1, cmax))
    for p in range(SB_STREAMS):
        o_ref[0, :, p * LANES:(p + 1) * LANES] = acc_ref[p]


def _sb_attention(proj3, u2, t):
    b, s, _ = proj3.shape
    groups = N_SB_PAIRS // SB_STREAMS
    width = SB_STREAMS * LANES
    return pl.pallas_call(
        functools.partial(_sb_kernel, t=t),
        grid=(b, groups, s // t),
        in_specs=[pl.BlockSpec((1, t, width), lambda bi, g, qi: (bi, qi, g)),
                  pl.BlockSpec((1, s, width), lambda bi, g, qi: (bi, 0, groups + g)),
                  pl.BlockSpec((1, s, width), lambda bi, g, qi: (bi, 0, 2 * groups + g)),
                  pl.BlockSpec((2 * t, t), lambda bi, g, qi: (0, 0))],
        out_specs=pl.BlockSpec((1, t, width), lambda bi, g, qi: (bi, qi, g)),
        out_shape=jax.ShapeDtypeStruct((b, s, SB_WIDTH), F32),
        scratch_shapes=[pltpu.VMEM((SB_STREAMS, t, LANES), F32), pltpu.VMEM((SB_STREAMS, 2 * t, LANES), F32)],
        compiler_params=_params("parallel", "parallel", "arbitrary"),
        name="sb_attention",
    )(proj3, proj3, proj3, u2)


def _df_kernel(q_ref, k_ref, v_ref, lam_ref, g_ref, o_ref, acc_ref, m_ref, z_ref, *, t, lambda_init):
    h = pl.program_id(1)
    qi = pl.program_id(2)
    slope = jnp.float32(ALIBI_SLOPES[-1])
    for idx in range(N_DF_HEADS - 2, -1, -1):
        slope = jnp.where(h == idx, jnp.float32(ALIBI_SLOPES[idx]), slope)
    lane = lax.broadcasted_iota(jnp.int32, (t, LANES), 1)
    row = lax.broadcasted_iota(jnp.int32, (t, LANES), 0)
    first = lane < HEAD_DIM
    q = q_ref[0] * SCALE
    zero = jnp.zeros_like(q)
    bias_on = jnp.where(lane < 2, 1.0, 0.0).astype(BF16)
    q2 = jnp.concatenate([jnp.concatenate([jnp.where(first, q, zero), bias_on], axis=1),
                          jnp.concatenate([jnp.where(first, zero, q), bias_on], axis=1)], axis=0)
    key_lo = jnp.where(lane == 0, ((row >> 8) << 8).astype(F32),
                       jnp.where(lane == 1, (row & 255).astype(F32), 0.0)) * slope
    lane0 = lane == 0
    ones_v = jnp.ones((t, LANES), BF16)

    def logits(kb):
        k = k_ref[0, pl.ds(pl.multiple_of(kb * t, t), t), :]
        offset = slope * ((kb - qi) * t).astype(F32)
        k_bias = (key_lo + jnp.where(lane0, offset, 0.0)).astype(BF16)
        return _dot_nt(q2, jnp.concatenate([k, k_bias], axis=1))

    def accumulate(z, kb):
        v = v_ref[0, pl.ds(pl.multiple_of(kb * t, t), t), :]
        m_old = m_ref[...]
        m_new = jnp.maximum(m_old, jnp.max(z, axis=-1, keepdims=True))
        alpha = jnp.exp(m_old - m_new)
        p = jnp.exp(z - jnp.concatenate([m_new] * (t // LANES), axis=1)).astype(BF16)
        pv = _dot(p, jnp.concatenate([v, ones_v], axis=1))
        acc_ref[...] = jnp.concatenate([alpha, alpha], axis=1) * acc_ref[...] + pv
        m_ref[...] = m_new

    acc_ref[...] = jnp.zeros_like(acc_ref)
    m_ref[...] = jnp.full_like(m_ref, -jnp.inf)
    r2 = lax.broadcasted_iota(jnp.int32, (2 * t, t), 0)
    c2 = lax.broadcasted_iota(jnp.int32, (2 * t, t), 1)
    z_ref[...] = jnp.where(c2 <= jnp.where(r2 >= t, r2 - t, r2), logits(qi), -jnp.inf)

    def body(j, carry):
        z = z_ref[...]
        z_ref[...] = logits(j)
        accumulate(z, jnp.where(j == 0, qi, j - 1))
        return carry

    lax.fori_loop(0, qi, body, 0)
    accumulate(z_ref[...], jnp.where(qi == 0, qi, qi - 1))

    lam_vec = lam_ref[...]
    lam = (jnp.exp(jnp.sum(lam_vec[0:1] * lam_vec[1:2], axis=-1, keepdims=True))
           - jnp.exp(jnp.sum(lam_vec[2:3] * lam_vec[3:4], axis=-1, keepdims=True))
           + lambda_init)
    acc = acc_ref[...]
    ratio = acc[:, :LANES] / acc[:, LANES:]
    o = ratio[:t] - lam * ratio[t:]
    o_ref[0] = (_rms(o, g_ref[...]) * (1.0 - lambda_init)).astype(o_ref.dtype)


def _df_attention(proj3, lam_rows, g_df, t, lambda_init):
    b, s, _ = proj3.shape
    col0 = 3 * N_SB_PAIRS
    return pl.pallas_call(
        functools.partial(_df_kernel, t=t, lambda_init=lambda_init),
        grid=(b, N_DF_HEADS, s // t),
        in_specs=[pl.BlockSpec((1, t, LANES), lambda bi, h, qi: (bi, qi, col0 + h)),
                  pl.BlockSpec((1, s, LANES), lambda bi, h, qi: (bi, 0, col0 + N_DF_HEADS + h)),
                  pl.BlockSpec((1, s, LANES), lambda bi, h, qi: (bi, 0, col0 + 2 * N_DF_HEADS + h)),
                  pl.BlockSpec((4, HEAD_DIM), lambda bi, h, qi: (0, 0)),
                  pl.BlockSpec((1, LANES), lambda bi, h, qi: (0, 0))],
        out_specs=pl.BlockSpec((1, t, LANES), lambda bi, h, qi: (bi, qi, h)),
        out_shape=jax.ShapeDtypeStruct((b, s, DF_WIDTH), BF16),
        scratch_shapes=[pltpu.VMEM((2 * t, 2 * LANES), F32), pltpu.VMEM((2 * t, LANES), F32),
                        pltpu.VMEM((2 * t, t), F32)],
        compiler_params=_params("parallel", "parallel", "arbitrary"),
        name="df_attention",
    )(proj3, proj3, proj3, lam_rows, g_df)


def _split_bf16(x):
    hi = x.astype(BF16)
    return hi, (x - hi.astype(F32)).astype(BF16)


def _route(logits):
    lane = lax.broadcasted_iota(jnp.int32, logits.shape, 1)
    neg = -jnp.inf
    big = jnp.int32(LANES)
    gl = jnp.where(lane < N_GROUPS, logits, neg)
    gmax = jnp.max(gl, axis=-1, keepdims=True)
    g_idx = jnp.min(jnp.where(gl == gmax, lane, big), axis=-1, keepdims=True)
    g_w = 1.0 / jnp.sum(jnp.exp(gl - gmax), axis=-1, keepdims=True)
    e_lane = lane - ROUTER_LANE0
    in_group = jnp.logical_and(jnp.logical_and(e_lane >= 0, e_lane < N_EXPERTS),
                               (e_lane // EXPERTS_PER_GROUP) == g_idx)
    v1 = jnp.where(in_group, logits, neg)
    t1 = jnp.max(v1, axis=-1, keepdims=True)
    i1 = jnp.min(jnp.where(v1 == t1, lane, big), axis=-1, keepdims=True)
    v2 = jnp.where(lane == i1, neg, v1)
    t2 = jnp.max(v2, axis=-1, keepdims=True)
    i2 = jnp.min(jnp.where(v2 == t2, lane, big), axis=-1, keepdims=True)
    e2 = jnp.exp(t2 - t1)
    w1 = g_w / (1.0 + e2)
    w2 = w1 * e2
    id1 = (i1 - ROUTER_LANE0).astype(F32)
    id2 = (i2 - ROUTER_LANE0).astype(F32)
    return jnp.where(lane == 0, id1, jnp.where(lane == 1, id2,
                                               jnp.where(lane == 2, w1, jnp.where(lane == 3, w2, 0.0))))


def _outproj_kernel(osb_ref, odf_ref, x_ref, gsb_ref, wsb_ref, wdf_ref, gffn_ref, wr_ref, br_ref,
                    x1_ref, h2_ref, route_ref):
    a_sb = _rms(osb_ref[...], gsb_ref[...]).astype(BF16)
    x1 = x_ref[...] + _dot(a_sb, wsb_ref[...]) + _dot(odf_ref[...], wdf_ref[...])
    x1_ref[...] = x1
    h2 = _rms(x1, gffn_ref[...])
    h2_ref[...] = h2
    hi, lo = _split_bf16(h2)
    logits = _rows_dot(_dot, jnp.concatenate([hi, hi, lo], axis=1), wr_ref[...], hi.shape[0] // 2) + br_ref[...]
    route_ref[...] = _route(logits)


def _outproj(o_sb, o_df, x2d, g_sb, w_sb, w_df, g_ffn, w_router3, b_router, tm):
    n, d = x2d.shape
    row = lambda i: (i, 0)
    const = lambda i: (0, 0)
    return pl.pallas_call(
        _outproj_kernel,
        grid=(n // tm,),
        in_specs=[pl.BlockSpec((tm, SB_WIDTH), row), pl.BlockSpec((tm, DF_WIDTH), row),
                  pl.BlockSpec((tm, d), row), pl.BlockSpec((1, SB_WIDTH), const),
                  pl.BlockSpec((SB_WIDTH, d), const), pl.BlockSpec((DF_WIDTH, d), const),
                  pl.BlockSpec((1, d), const), pl.BlockSpec((3 * d, LANES), const),
                  pl.BlockSpec((1, LANES), const)],
        out_specs=[pl.BlockSpec((tm, d), row), pl.BlockSpec((tm, d), row),
                   pl.BlockSpec((tm, LANES), row)],
        out_shape=[jax.ShapeDtypeStruct((n, d), F32), jax.ShapeDtypeStruct((n, d), F32),
                   jax.ShapeDtypeStruct((n, LANES), F32)],
        compiler_params=_params("parallel"),
        name="outproj_router",
    )(o_sb, o_df, x2d, g_sb, w_sb, w_df, g_ffn, w_router3, b_router)


META_ROWS = 8


def _plan_kernel(route_ref, pos_ref, meta_ref, cnt_ref, run_ref, off_ref, *, tm, tile_rows):
    phase = pl.program_id(0)
    i = pl.program_id(1)
    route = route_ref[...]
    lane = lax.broadcasted_iota(jnp.int32, (tm, LANES), 1)
    lane_f = lane.astype(F32)
    sel1 = lane_f == route[:, 0:1]
    sel2 = lane_f == route[:, 1:2]
    onehot = jnp.where(jnp.logical_or(sel1, sel2), 1.0, 0.0)
    tile_count = jnp.sum(onehot, axis=0, keepdims=True)

    @pl.when(jnp.logical_and(phase == 0, i == 0))
    def _():
        cnt_ref[...] = jnp.zeros_like(cnt_ref)

    @pl.when(phase == 0)
    def _():
        cnt_ref[...] += tile_count

    @pl.when(jnp.logical_and(phase == 1, i == 0))
    def _():
        cnt = cnt_ref[...]
        n_tiles = jnp.ceil(cnt * (1.0 / tile_rows))
        r = lax.broadcasted_iota(jnp.int32, (LANES, LANES), 0)
        c = lax.broadcasted_iota(jnp.int32, (LANES, LANES), 1)
        tile_off = _dot(n_tiles.astype(BF16), jnp.where(r < c, 1.0, 0.0).astype(BF16))
        off_ref[...] = tile_off * tile_rows
        run_ref[...] = jnp.zeros_like(run_ref)
        mrow = lax.broadcasted_iota(jnp.int32, (META_ROWS, LANES), 0)
        meta_ref[...] = jnp.where(mrow == 0, tile_off, jnp.where(mrow == 1, n_tiles,
                                                                 jnp.where(mrow == 2, cnt, 0.0)))

    @pl.when(phase == 1)
    def _():
        r = lax.broadcasted_iota(jnp.int32, (tm, tm), 0)
        c = lax.broadcasted_iota(jnp.int32, (tm, tm), 1)
        earlier = _dot(jnp.where(c < r, 1.0, 0.0).astype(BF16), onehot.astype(BF16))
        base = earlier + run_ref[0:1] + off_ref[0:1]
        pos1 = jnp.sum(jnp.where(sel1, base, 0.0), axis=-1, keepdims=True)
        pos2 = jnp.sum(jnp.where(sel2, base, 0.0), axis=-1, keepdims=True)
        pos_ref[...] = jnp.where(lane == 0, pos1, jnp.where(lane == 1, pos2, 0.0)).astype(jnp.int32)
        run_ref[...] += tile_count


def _plan(route, tm, tile_rows):
    n = route.shape[0]
    assert n // tile_rows + 1 <= 256, "per-expert tile counts must stay exact in bf16"
    small = pltpu.VMEM((META_ROWS, LANES), F32)
    return pl.pallas_call(
        functools.partial(_plan_kernel, tm=tm, tile_rows=tile_rows),
        grid=(2, n // tm),
        in_specs=[pl.BlockSpec((tm, LANES), lambda ph, i: (i, 0))],
        out_specs=[pl.BlockSpec((tm, LANES), lambda ph, i: (i * ph, 0)),
                   pl.BlockSpec((META_ROWS, LANES), lambda ph, i: (0, 0))],
        out_shape=[jax.ShapeDtypeStruct((n, LANES), jnp.int32),
                   jax.ShapeDtypeStruct((META_ROWS, LANES), F32)],
        scratch_shapes=[small, small, small],
        compiler_params=_params("arbitrary", "arbitrary"),
        name="moe_plan",
    )(route)


ROW_GROUP = 8


def _row_copy(src_ref, src_row, dst_ref, dst_row, sem):
    return pltpu.make_async_copy(src_ref.at[pl.ds(src_row, 1)], dst_ref.at[pl.ds(dst_row, 1)], sem)


def _dispatch_kernel(meta_ref, h_ref, pos_ref, xs_ref, zero_ref, sem, *, tm, tile_rows):
    i = pl.program_id(0)

    def start(g, carry):
        base = pl.multiple_of(g * ROW_GROUP, ROW_GROUP)
        for j in range(ROW_GROUP):
            for slot in range(2):
                dst = pos_ref[0, 0, 2 * base + (2 * j + slot)]
                _row_copy(h_ref, base + j, xs_ref, dst, sem).start(priority=slot)
        return carry

    def wait(r, carry):
        _row_copy(h_ref, 0, xs_ref, 0, sem).wait()
        return carry

    lax.fori_loop(0, tm // ROW_GROUP, start, 0)
    lax.fori_loop(0, 2 * tm, wait, 0, unroll=8)

    @pl.when(i == pl.num_programs(0) - 1)
    def _():
        zero_ref[...] = jnp.zeros_like(zero_ref)

        def per_expert(e, carry):
            off = meta_ref[0, e] * tile_rows
            lo = meta_ref[2, e]
            hi = meta_ref[1, e] * tile_rows

            def fill(r, c):
                _row_copy(zero_ref, 0, xs_ref, off + r, sem).start()
                return c

            def drain(r, c):
                _row_copy(zero_ref, 0, xs_ref, 0, sem).wait()
                return c

            lax.fori_loop(lo, hi, fill, 0)
            lax.fori_loop(lo, hi, drain, 0)
            return carry

        lax.fori_loop(0, N_EXPERTS, per_expert, 0)

        def spare_tile(j, carry):
            copy = pltpu.make_async_copy(zero_ref, xs_ref.at[pl.ds(j * tile_rows, tile_rows)], sem)
            copy.start()
            copy.wait()
            return carry

        n_used = meta_ref[0, N_EXPERTS - 1] + meta_ref[1, N_EXPERTS - 1]
        lax.fori_loop(n_used, xs_ref.shape[0] // tile_rows, spare_tile, 0)


def _dispatch(meta_i, h2, pos3, n_rows, tm, tile_rows):
    n, d = h2.shape
    return pl.pallas_call(
        functools.partial(_dispatch_kernel, tm=tm, tile_rows=tile_rows),
        grid_spec=pltpu.PrefetchScalarGridSpec(
            num_scalar_prefetch=1,
            grid=(n // tm,),
            in_specs=[pl.BlockSpec((tm, d), lambda i, meta: (i, 0)),
                      pl.BlockSpec((1, 1, 2 * tm), lambda i, meta: (i, 0, 0), memory_space=pltpu.SMEM)],
            out_specs=pl.BlockSpec(memory_space=pl.ANY),
            scratch_shapes=[pltpu.VMEM((tile_rows, d), F32), pltpu.SemaphoreType.DMA(())]),
        out_shape=jax.ShapeDtypeStruct((n_rows, d), F32),
        compiler_params=_params("arbitrary"),
        name="moe_dispatch",
    )(meta_i, h2, pos3)


def _gmm_kernel(te_ref, nv_ref, x_ref, wg_ref, wu_ref, wd_ref, y_ref):
    used = pl.program_id(0) < nv_ref[0]

    @pl.when(used)
    def _():
        x = x_ref[...].astype(BF16)
        a = _dot(x, wg_ref[0])
        hid = (a * jax.nn.sigmoid(a)) * _dot(x, wu_ref[0])
        y_ref[...] = _dot(hid.astype(BF16), wd_ref[0])

    @pl.when(jnp.logical_not(used))
    def _():
        y_ref[...] = jnp.zeros_like(y_ref)


def _gmm(tile_expert, n_valid, xs, wg, wu, wd, tile_rows):
    n_rows, d = xs.shape
    de = wg.shape[2]
    rows = lambda j, te, nv: (jnp.minimum(j, nv[0] - 1), 0)
    return pl.pallas_call(
        _gmm_kernel,
        grid_spec=pltpu.PrefetchScalarGridSpec(
            num_scalar_prefetch=2,
            grid=(n_rows // tile_rows,),
            in_specs=[pl.BlockSpec((tile_rows, d), rows),
                      pl.BlockSpec((1, d, de), lambda j, te, nv: (te[j], 0, 0)),
                      pl.BlockSpec((1, d, de), lambda j, te, nv: (te[j], 0, 0)),
                      pl.BlockSpec((1, de, d), lambda j, te, nv: (te[j], 0, 0))],
            out_specs=pl.BlockSpec((tile_rows, d), lambda j, te, nv: (j, 0))),
        out_shape=jax.ShapeDtypeStruct((n_rows, d), F32),
        compiler_params=_params("arbitrary"),
        name="moe_experts",
    )(tile_expert, n_valid, xs, wg, wu, wd)


def _ple_kernel(x1_ref, p_ref, route_ref, pos_ref, pos_next_ref, ys_ref, gple_ref, wgate_ref, wproj_ref, gfin_ref,
                o_ref, y_ref, sem, *, tm):
    i = pl.program_id(0)
    buf = lax.rem(i, 2)

    def gather(pos, b):
        def start(g, carry):
            base = pl.multiple_of(g * ROW_GROUP, ROW_GROUP)
            for j in range(ROW_GROUP):
                for slot in range(2):
                    src = pos[0, 0, 2 * base + (2 * j + slot)]
                    _row_copy(ys_ref, src, y_ref.at[b, slot], base + j, sem.at[b]).start(priority=slot)
            return carry

        lax.fori_loop(0, tm // ROW_GROUP, start, 0)

    @pl.when(i == 0)
    def _():
        gather(pos_ref, 0)

    @pl.when(i + 1 < pl.num_programs(0))
    def _():
        gather(pos_next_ref, 1 - buf)

    def wait(r, carry):
        _row_copy(ys_ref, 0, y_ref.at[buf, 0], 0, sem.at[buf]).wait()
        return carry

    emb = _dot(p_ref[...].astype(BF16), wproj_ref[...])
    lax.fori_loop(0, 2 * tm, wait, 0, unroll=8)
    route = route_ref[...]
    x = x1_ref[...] + (route[:, 2:3] * y_ref[buf, 0] + route[:, 3:4] * y_ref[buf, 1])
    gate = jax.nn.sigmoid(_dot(_rms(x, gple_ref[...]).astype(BF16), wgate_ref[...]))
    o_ref[...] = _rms(x + gate * emb, gfin_ref[...])


def _ple(x1, p2d, route, pos3, ys, g_ple, w_gate, w_proj, g_final, tm):
    n, d = x1.shape
    pd = p2d.shape[1]
    steps = n // tm
    row = lambda i: (i, 0)
    const = lambda i: (0, 0)
    pos_spec = lambda index_map: pl.BlockSpec((1, 1, 2 * tm), index_map, memory_space=pltpu.SMEM)
    return pl.pallas_call(
        functools.partial(_ple_kernel, tm=tm),
        grid=(steps,),
        in_specs=[pl.BlockSpec((tm, d), row), pl.BlockSpec((tm, pd), row), pl.BlockSpec((tm, LANES), row),
                  pos_spec(lambda i: (i, 0, 0)), pos_spec(lambda i: (jnp.minimum(i + 1, steps - 1), 0, 0)),
                  pl.BlockSpec(memory_space=pl.ANY), pl.BlockSpec((1, d), const),
                  pl.BlockSpec((d, d), const), pl.BlockSpec((pd, d), const), pl.BlockSpec((1, d), const)],
        out_specs=pl.BlockSpec((tm, d), row),
        out_shape=jax.ShapeDtypeStruct((n, d), F32),
        scratch_shapes=[pltpu.VMEM((2, 2, tm, d), F32), pltpu.SemaphoreType.DMA((2,))],
        compiler_params=_params("arbitrary"),
        name="combine_ple_final",
    )(x1, p2d, route, pos3, pos3, ys, g_ple, w_gate, w_proj, g_final)


def _layer(x, p_i, layer_idx, g_mix, w_in, lambda_q1, lambda_k1, lambda_q2, lambda_k2, g_sb_out, g_df_out,
           w_out, g_ffn, w_router_group, b_router_group, w_router_expert, b_router_expert, w_expert_gate,
           w_expert_up, w_expert_down, g_ple, w_ple_gate, w_ple_proj, g_out):
    b, s, d = x.shape
    n = b * s
    tm = min(512, n)
    t = min(256, s)
    x2d = x.reshape(n, d)
    lambda_init = 0.8 - 0.6 * math.exp(-0.3 * layer_idx)

    proj = _inproj(x2d, g_mix.reshape(1, d), w_in.astype(BF16), tm)
    proj3 = proj.reshape(b, s, proj.shape[1])

    tri = (lax.broadcasted_iota(jnp.int32, (t, t), 0) > lax.broadcasted_iota(jnp.int32, (t, t), 1))
    u2 = jnp.concatenate([tri, tri], axis=0).astype(BF16)
    o_sb = _sb_attention(proj3, u2, t).reshape(n, SB_WIDTH)

    lam_rows = jnp.stack([lambda_q1, lambda_k1, lambda_q2, lambda_k2]).astype(F32)
    o_df = _df_attention(proj3, lam_rows, g_df_out.reshape(1, LANES).astype(F32), min(512, s),
                         lambda_init).reshape(n, DF_WIDTH)

    w_router = jnp.concatenate([w_router_group, w_router_expert], axis=1).astype(F32)
    w_router = jnp.pad(w_router, ((0, 0), (0, LANES - w_router.shape[1])))
    wr_hi, wr_lo = _split_bf16(w_router)
    b_router = jnp.pad(jnp.concatenate([b_router_group, b_router_expert]).astype(F32),
                       (0, LANES - N_GROUPS - N_EXPERTS)).reshape(1, LANES)
    w_out_bf = w_out.astype(BF16)
    x1, h2, route = _outproj(o_sb, o_df, x2d, g_sb_out.reshape(1, SB_WIDTH), w_out_bf[:SB_WIDTH],
                             w_out_bf[SB_WIDTH:], g_ffn.reshape(1, d),
                             jnp.concatenate([wr_hi, wr_lo, wr_hi], axis=0), b_router, tm)

    tile_rows = min(MOE_TILE_ROWS, n)
    pos, meta = _plan(route, tm, tile_rows)
    n_tiles = 2 * n // tile_rows + N_EXPERTS
    meta_i = meta[:3, :N_EXPERTS].astype(jnp.int32)
    ends = meta_i[0] + meta_i[1]
    n_valid = ends[-1:]
    tile_ids = jnp.arange(n_tiles, dtype=jnp.int32)
    tile_expert = jnp.sum(tile_ids[:, None] >= ends[None, :], axis=1).astype(jnp.int32)
    tile_expert = jnp.where(tile_ids < n_valid, tile_expert, tile_expert[n_valid[0] - 1])
    pos2 = pos[:, :2]
    xs = _dispatch(meta_i, h2, pos2.reshape(n // tm, 1, 2 * tm), n_tiles * tile_rows, tm, tile_rows)
    ys = _gmm(tile_expert, n_valid, xs, w_expert_gate.astype(BF16), w_expert_up.astype(BF16),
              w_expert_down.astype(BF16), tile_rows)

    tc = min(COMBINE_ROWS, n)
    out = _ple(x1, p_i.reshape(n, p_i.shape[-1]), route, pos2.reshape(n // tc, 1, 2 * tc), ys,
               g_ple.reshape(1, d), w_ple_gate.astype(BF16), w_ple_proj.astype(BF16), g_out.reshape(1, d), tc)
    return out.reshape(b, s, d)


def kernel(x, p, g_mix, w_in, lambda_q1, lambda_k1, lambda_q2, lambda_k2, g_sb_out, g_df_out, w_out, g_ffn,
           w_router_group, b_router_group, w_router_expert, b_router_expert, w_expert_gate, w_expert_up,
           w_expert_down, g_ple, w_ple_gate, w_ple_proj, g_final):
    depth = p.shape[0]
    assert depth == 1, "the final norm is fused into the single layer's last kernel"
    return _layer(x, p[0], 0, g_mix[0], w_in[0], lambda_q1[0], lambda_k1[0], lambda_q2[0], lambda_k2[0],
                  g_sb_out[0], g_df_out[0], w_out[0], g_ffn[0], w_router_group[0], b_router_group[0],
                  w_router_expert[0], b_router_expert[0], w_expert_gate[0], w_expert_up[0],
                  w_expert_down[0], g_ple[0], w_ple_gate[0], w_ple_proj[0], g_final)
```

```python
import functools
import math

import jax
import jax.numpy as jnp
from jax import lax
from jax.experimental import pallas as pl
from jax.experimental.pallas import tpu as pltpu

F32 = jnp.float32
BF16 = jnp.bfloat16

HEAD_DIM = 64
LANES = 128
N_SB_PAIRS = 4
N_DF_HEADS = 4
SB_WIDTH = 512
DF_WIDTH = 512
SCALE = HEAD_DIM ** -0.5
NORM_EPS = 1e-6
N_GROUPS = 4
EXPERTS_PER_GROUP = 4
N_EXPERTS = 16
ROUTER_LANE0 = N_GROUPS
ALIBI_SLOPES = tuple(2.0 ** (-8.0 * (h + 1) / N_DF_HEADS) for h in range(N_DF_HEADS))
SB_LOG_ZERO = -110.0
VMEM_LIMIT = 48 * 1024 * 1024
MOE_TILE_ROWS = 512
COMBINE_ROWS = 256


def _rms(x, g):
    ms = jnp.mean(x * x, axis=-1, keepdims=True)
    return x * lax.rsqrt(ms + NORM_EPS) * g


def _dot(a, b):
    return jnp.dot(a, b, preferred_element_type=F32)


def _dot_nt(a, b):
    return lax.dot_general(a, b, (((1,), (1,)), ((), ())), preferred_element_type=F32)


def _rows_dot(dot, a, b, rows):
    return jnp.concatenate([dot(a[r:r + rows], b) for r in range(0, a.shape[0], rows)], axis=0)


def _params(*sem):
    return pltpu.CompilerParams(dimension_semantics=sem, vmem_limit_bytes=VMEM_LIMIT)


def _inproj_kernel(x_ref, g_ref, w_ref, o_ref, *, tn):
    h = _rms(x_ref[...], g_ref[...]).astype(BF16)
    for j in range(o_ref.shape[1] // tn):
        o_ref[:, j * tn:(j + 1) * tn] = _dot(h, w_ref[:, j * tn:(j + 1) * tn]).astype(o_ref.dtype)


def _inproj(x2d, g, w_bf16, tm):
    n, d = x2d.shape
    width = w_bf16.shape[1]
    return pl.pallas_call(
        functools.partial(_inproj_kernel, tn=1024),
        grid=(n // tm,),
        in_specs=[pl.BlockSpec((tm, d), lambda i: (i, 0)),
                  pl.BlockSpec((1, d), lambda i: (0, 0)),
                  pl.BlockSpec((d, width), lambda i: (0, 0))],
        out_specs=pl.BlockSpec((tm, width), lambda i: (i, 0)),
        out_shape=jax.ShapeDtypeStruct((n, width), BF16),
        compiler_params=_params("parallel"),
        name="inproj",
    )(x2d, g, w_bf16)


SB_STREAMS = 2


def _sb_kernel(q_ref, k_ref, v_ref, u_ref, o_ref, acc_ref, c_ref, *, t):
    qi = pl.program_id(2)
    lane = lax.broadcasted_iota(jnp.int32, (t, LANES), 1)
    first = lane < HEAD_DIM

    def stacked_q(p):
        q = q_ref[0, :, p * LANES:(p + 1) * LANES] * SCALE
        zero = jnp.zeros_like(q)
        return jnp.concatenate([jnp.where(first, q, zero), jnp.where(first, zero, q)], axis=0)

    q2 = [stacked_q(p) for p in range(SB_STREAMS)]

    def pair_block(p, kb, strict_mask):
        start = pl.multiple_of(kb * t, t)
        k = k_ref[0, pl.ds(start, t), p * LANES:(p + 1) * LANES]
        v = v_ref[0, pl.ds(start, t), p * LANES:(p + 1) * LANES]
        z = _rows_dot(_dot_nt, q2[p], k, t)
        soft = jnp.log(1.0 + jnp.exp(-jnp.abs(z)))
        log_beta = jnp.minimum(z, 0.0) - soft
        log_keep = log_beta - z
        if strict_mask is not None:
            log_keep = jnp.where(strict_mask, log_keep, 0.0)
        hi = log_keep.astype(BF16)
        lo = (log_keep - hi.astype(F32)).astype(BF16)
        rev = _rows_dot(_dot, jnp.concatenate([hi, lo], axis=1), u_ref[...], t)
        c = c_ref[p]
        w = jnp.exp(log_beta + rev + jnp.concatenate([c] * (t // LANES), axis=1))
        if strict_mask is not None:
            w = jnp.where(strict_mask, w, 0.0)
        w = w.astype(BF16)
        vz = jnp.zeros_like(v)
        v2 = jnp.concatenate([jnp.where(first, v, vz), jnp.where(first, vz, v)], axis=0)
        acc_ref[p] += _dot(jnp.concatenate([w[:t], w[t:]], axis=1), v2)
        c_new = c + jnp.sum(log_keep, axis=-1, keepdims=True)
        c_ref[p] = c_new
        return jnp.max(c_new)

    def block(kb, strict_mask):
        worst = pair_block(0, kb, strict_mask)
        for p in range(1, SB_STREAMS):
            worst = jnp.maximum(worst, pair_block(p, kb, strict_mask))
        return worst

    acc_ref[...] = jnp.zeros_like(acc_ref)
    c_ref[...] = jnp.zeros_like(c_ref)
    row = lax.broadcasted_iota(jnp.int32, (2 * t, t), 0)
    col = lax.broadcasted_iota(jnp.int32, (2 * t, t), 1)
    strict = col < jnp.where(row >= t, row - t, row)
    cmax = block(qi, strict)

    def cond(carry):
        kb, cm = carry
        return jnp.logical_and(kb >= 0, cm > SB_LOG_ZERO)

    def body(carry):
        kb, _ = carry
        return kb - 1, block(kb, None)

    lax.while_loop(cond, body, (qi - 1, cmax))
    for p in range(SB_STREAMS):
        o_ref[0, :, p * LANES:(p + 1) * LANES] = acc_ref[p]


def _sb_attention(proj3, u2, t):
    b, s, _ = proj3.shape
    groups = N_SB_PAIRS // SB_STREAMS
    width = SB_STREAMS * LANES
    return pl.pallas_call(
        functools.partial(_sb_kernel, t=t),
        grid=(b, groups, s // t),
        in_specs=[pl.BlockSpec((1, t, width), lambda bi, g, qi: (bi, qi, g)),
                  pl.BlockSpec((1, s, width), lambda bi, g, qi: (bi, 0, groups + g)),
                  pl.BlockSpec((1, s, width), lambda bi, g, qi: (bi, 0, 2 * groups + g)),
                  pl.BlockSpec((2 * t, t), lambda bi, g, qi: (0, 0))],
        out_specs=pl.BlockSpec((1, t, width), lambda bi, g, qi: (bi, qi, g)),
        out_shape=jax.ShapeDtypeStruct((b, s, SB_WIDTH), F32),
        scratch_shapes=[pltpu.VMEM((SB_STREAMS, t, LANES), F32), pltpu.VMEM((SB_STREAMS, 2 * t, LANES), F32)],
        compiler_params=_params("parallel", "parallel", "arbitrary"),
        name="sb_attention",
    )(proj3, proj3, proj3, u2)


def _df_kernel(q_ref, k_ref, v_ref, lam_ref, g_ref, o_ref, acc_ref, m_ref, z_ref, *, t, lambda_init):
    h = pl.program_id(1)
    qi = pl.program_id(2)
    slope = jnp.float32(ALIBI_SLOPES[-1])
    for idx in range(N_DF_HEADS - 2, -1, -1):
        slope = jnp.where(h == idx, jnp.float32(ALIBI_SLOPES[idx]), slope)
    lane = lax.broadcasted_iota(jnp.int32, (t, LANES), 1)
    row = lax.broadcasted_iota(jnp.int32, (t, LANES), 0)
    first = lane < HEAD_DIM
    q = q_ref[0] * SCALE
    zero = jnp.zeros_like(q)
    bias_on = jnp.where(lane < 2, 1.0, 0.0).astype(BF16)
    q2 = jnp.concatenate([jnp.concatenate([jnp.where(first, q, zero), bias_on], axis=1),
                          jnp.concatenate([jnp.where(first, zero, q), bias_on], axis=1)], axis=0)
    key_lo = jnp.where(lane == 0, ((row >> 8) << 8).astype(F32),
                       jnp.where(lane == 1, (row & 255).astype(F32), 0.0)) * slope
    lane0 = lane == 0
    ones_v = jnp.ones((t, LANES), BF16)

    def logits(kb):
        k = k_ref[0, pl.ds(pl.multiple_of(kb * t, t), t), :]
        offset = slope * ((kb - qi) * t).astype(F32)
        k_bias = (key_lo + jnp.where(lane0, offset, 0.0)).astype(BF16)
        return _dot_nt(q2, jnp.concatenate([k, k_bias], axis=1))

    def accumulate(z, kb):
        v = v_ref[0, pl.ds(pl.multiple_of(kb * t, t), t), :]
        m_old = m_ref[...]
        m_new = jnp.maximum(m_old, jnp.max(z, axis=-1, keepdims=True))
        alpha = jnp.exp(m_old - m_new)
        p = jnp.exp(z - jnp.concatenate([m_new] * (t // LANES), axis=1)).astype(BF16)
        pv = _dot(p, jnp.concatenate([v, ones_v], axis=1))
        acc_ref[...] = jnp.concatenate([alpha, alpha], axis=1) * acc_ref[...] + pv
        m_ref[...] = m_new

    acc_ref[...] = jnp.zeros_like(acc_ref)
    m_ref[...] = jnp.full_like(m_ref, -jnp.inf)
    r2 = lax.broadcasted_iota(jnp.int32, (2 * t, t), 0)
    c2 = lax.broadcasted_iota(jnp.int32, (2 * t, t), 1)
    z_ref[...] = jnp.where(c2 <= jnp.where(r2 >= t, r2 - t, r2), logits(qi), -jnp.inf)

    def body(j, carry):
        z = z_ref[...]
        z_ref[...] = logits(j)
        accumulate(z, jnp.where(j == 0, qi, j - 1))
        return carry

    lax.fori_loop(0, qi, body, 0)
    accumulate(z_ref[...], jnp.where(qi == 0, qi, qi - 1))

    lam_vec = lam_ref[...]
    lam = (jnp.exp(jnp.sum(lam_vec[0:1] * lam_vec[1:2], axis=-1, keepdims=True))
           - jnp.exp(jnp.sum(lam_vec[2:3] * lam_vec[3:4], axis=-1, keepdims=True))
           + lambda_init)
    acc = acc_ref[...]
    ratio = acc[:, :LANES] / acc[:, LANES:]
    o = ratio[:t] - lam * ratio[t:]
    o_ref[0] = (_rms(o, g_ref[...]) * (1.0 - lambda_init)).astype(o_ref.dtype)


def _df_attention(proj3, lam_rows, g_df, t, lambda_init):
    b, s, _ = proj3.shape
    col0 = 3 * N_SB_PAIRS
    return pl.pallas_call(
        functools.partial(_df_kernel, t=t, lambda_init=lambda_init),
        grid=(b, N_DF_HEADS, s // t),
        in_specs=[pl.BlockSpec((1, t, LANES), lambda bi, h, qi: (bi, qi, col0 + h)),
                  pl.BlockSpec((1, s, LANES), lambda bi, h, qi: (bi, 0, col0 + N_DF_HEADS + h)),
                  pl.BlockSpec((1, s, LANES), lambda bi, h, qi: (bi, 0, col0 + 2 * N_DF_HEADS + h)),
                  pl.BlockSpec((4, HEAD_DIM), lambda bi, h, qi: (0, 0)),
                  pl.BlockSpec((1, LANES), lambda bi, h, qi: (0, 0))],
        out_specs=pl.BlockSpec((1, t, LANES), lambda bi, h, qi: (bi, qi, h)),
        out_shape=jax.ShapeDtypeStruct((b, s, DF_WIDTH), BF16),
        scratch_shapes=[pltpu.VMEM((2 * t, 2 * LANES), F32), pltpu.VMEM((2 * t, LANES), F32),
                        pltpu.VMEM((2 * t, t), F32)],
        compiler_params=_params("parallel", "parallel", "arbitrary"),
        name="df_attention",
    )(proj3, proj3, proj3, lam_rows, g_df)


def _split_bf16(x):
    hi = x.astype(BF16)
    return hi, (x - hi.astype(F32)).astype(BF16)


def _route(logits):
    lane = lax.broadcasted_iota(jnp.int32, logits.shape, 1)
    neg = -jnp.inf
    big = jnp.int32(LANES)
    gl = jnp.where(lane < N_GROUPS, logits, neg)
    gmax = jnp.max(gl, axis=-1, keepdims=True)
    g_idx = jnp.min(jnp.where(gl == gmax, lane, big), axis=-1, keepdims=True)
    g_w = 1.0 / jnp.sum(jnp.exp(gl - gmax), axis=-1, keepdims=True)
    e_lane = lane - ROUTER_LANE0
    in_group = jnp.logical_and(jnp.logical_and(e_lane >= 0, e_lane < N_EXPERTS),
                               (e_lane // EXPERTS_PER_GROUP) == g_idx)
    v1 = jnp.where(in_group, logits, neg)
    t1 = jnp.max(v1, axis=-1, keepdims=True)
    i1 = jnp.min(jnp.where(v1 == t1, lane, big), axis=-1, keepdims=True)
    v2 = jnp.where(lane == i1, neg, v1)
    t2 = jnp.max(v2, axis=-1, keepdims=True)
    i2 = jnp.min(jnp.where(v2 == t2, lane, big), axis=-1, keepdims=True)
    e2 = jnp.exp(t2 - t1)
    w1 = g_w / (1.0 + e2)
    w2 = w1 * e2
    id1 = (i1 - ROUTER_LANE0).astype(F32)
    id2 = (i2 - ROUTER_LANE0).astype(F32)
    return jnp.where(lane == 0, id1, jnp.where(lane == 1, id2,
                                               jnp.where(lane == 2, w1, jnp.where(lane == 3, w2, 0.0))))


def _outproj_kernel(osb_ref, odf_ref, x_ref, gsb_ref, wsb_ref, wdf_ref, gffn_ref, wr_ref, br_ref,
                    x1_ref, h2_ref, route_ref):
    a_sb = _rms(osb_ref[...], gsb_ref[...]).astype(BF16)
    x1 = x_ref[...] + _dot(a_sb, wsb_ref[...]) + _dot(odf_ref[...], wdf_ref[...])
    x1_ref[...] = x1
    h2 = _rms(x1, gffn_ref[...])
    h2_ref[...] = h2.astype(BF16)
    hi, lo = _split_bf16(h2)
    logits = _rows_dot(_dot, jnp.concatenate([hi, hi, lo], axis=1), wr_ref[...], hi.shape[0] // 2) + br_ref[...]
    route_ref[...] = _route(logits)


def _outproj(o_sb, o_df, x2d, g_sb, w_sb, w_df, g_ffn, w_router3, b_router, tm):
    n, d = x2d.shape
    row = lambda i: (i, 0)
    const = lambda i: (0, 0)
    return pl.pallas_call(
        _outproj_kernel,
        grid=(n // tm,),
        in_specs=[pl.BlockSpec((tm, SB_WIDTH), row), pl.BlockSpec((tm, DF_WIDTH), row),
                  pl.BlockSpec((tm, d), row), pl.BlockSpec((1, SB_WIDTH), const),
                  pl.BlockSpec((SB_WIDTH, d), const), pl.BlockSpec((DF_WIDTH, d), const),
                  pl.BlockSpec((1, d), const), pl.BlockSpec((3 * d, LANES), const),
                  pl.BlockSpec((1, LANES), const)],
        out_specs=[pl.BlockSpec((tm, d), row), pl.BlockSpec((tm, d), row),
                   pl.BlockSpec((tm, LANES), row)],
        out_shape=[jax.ShapeDtypeStruct((n, d), F32), jax.ShapeDtypeStruct((n, d), BF16),
                   jax.ShapeDtypeStruct((n, LANES), F32)],
        compiler_params=_params("parallel"),
        name="outproj_router",
    )(o_sb, o_df, x2d, g_sb, w_sb, w_df, g_ffn, w_router3, b_router)


META_ROWS = 8
ROW_GROUP = 8
RUN_SIZES = (512, 256, 128, 64, 32, 16, 8)


def _lane_prefix(x_groups):
    r = lax.broadcasted_iota(jnp.int32, (LANES, LANES), 0)
    c = lax.broadcasted_iota(jnp.int32, (LANES, LANES), 1)
    return _dot(x_groups.astype(BF16), jnp.where(r < c, 1.0, 0.0).astype(BF16))


def _plan_kernel(route_ref, pos_ref, meta_ref, seg_ref, cnt_ref, run_ref, off_ref, *, tm, tile_rows):
    phase = pl.program_id(0)
    i = pl.program_id(1)
    route = route_ref[...]
    lane = lax.broadcasted_iota(jnp.int32, (tm, LANES), 1)
    lane_f = lane.astype(F32)
    sel1 = lane_f == route[:, 0:1]
    sel2 = lane_f == route[:, 1:2]
    onehot = jnp.where(sel1, 1.0, jnp.where(sel2, 1.0, 0.0))
    run_groups = jnp.ceil(jnp.sum(onehot, axis=0, keepdims=True) * (1.0 / ROW_GROUP))
    run_rows = run_groups * ROW_GROUP

    @pl.when(jnp.logical_and(phase == 0, i == 0))
    def _():
        cnt_ref[...] = jnp.zeros_like(cnt_ref)

    @pl.when(phase == 0)
    def _():
        cnt_ref[...] += run_rows

    @pl.when(jnp.logical_and(phase == 1, i == 0))
    def _():
        cnt = cnt_ref[...]
        n_tiles = jnp.ceil(cnt * (1.0 / tile_rows))
        tile_off = _lane_prefix(n_tiles)
        off_ref[...] = tile_off * tile_rows
        run_ref[...] = jnp.zeros_like(run_ref)
        mrow = lax.broadcasted_iota(jnp.int32, (META_ROWS, LANES), 0)
        meta_ref[...] = jnp.where(mrow == 0, tile_off, jnp.where(mrow == 1, n_tiles,
                                                                 jnp.where(mrow == 2, cnt, 0.0)))

    @pl.when(phase == 1)
    def _():
        r = lax.broadcasted_iota(jnp.int32, (tm, tm), 0)
        c = lax.broadcasted_iota(jnp.int32, (tm, tm), 1)
        earlier = _dot(jnp.where(c < r, 1.0, 0.0).astype(BF16), onehot.astype(BF16))
        local_start = _lane_prefix(jnp.broadcast_to(run_groups, (META_ROWS, LANES))) * ROW_GROUP
        global_start = run_ref[...] + off_ref[...]
        base_g = earlier + global_start[0:1]
        base_l = earlier + local_start[0:1]

        def pick(sel, base):
            return jnp.sum(jnp.where(sel, base, 0.0), axis=-1, keepdims=True)

        pos_ref[...] = jnp.where(
            lane == 0, pick(sel1, base_g), jnp.where(
                lane == 1, pick(sel2, base_g), jnp.where(
                    lane == 2, pick(sel1, base_l), jnp.where(
                        lane == 3, pick(sel2, base_l), 0.0)))).astype(jnp.int32)
        mrow = lax.broadcasted_iota(jnp.int32, (META_ROWS, LANES), 0)
        seg_ref[...] = jnp.where(mrow == 0, local_start, jnp.where(
            mrow == 1, run_rows, jnp.where(mrow == 2, global_start, 0.0))).astype(jnp.int32)
        run_ref[...] += run_rows


def _plan(route, tm, tile_rows):
    n = route.shape[0]
    steps = n // tm
    assert (n + (ROW_GROUP - 1) * steps) // tile_rows + 1 <= 256, "per-expert tile counts must stay exact in bf16"
    assert (2 * tm) // ROW_GROUP + N_EXPERTS <= 256, "per-tile run sizes must stay exact in bf16"
    small = pltpu.VMEM((META_ROWS, LANES), F32)
    return pl.pallas_call(
        functools.partial(_plan_kernel, tm=tm, tile_rows=tile_rows),
        grid=(2, steps),
        in_specs=[pl.BlockSpec((tm, LANES), lambda ph, i: (i, 0))],
        out_specs=[pl.BlockSpec((tm, LANES), lambda ph, i: (i * ph, 0)),
                   pl.BlockSpec((META_ROWS, LANES), lambda ph, i: (0, 0)),
                   pl.BlockSpec((META_ROWS, LANES), lambda ph, i: (i * ph, 0))],
        out_shape=[jax.ShapeDtypeStruct((n, LANES), jnp.int32),
                   jax.ShapeDtypeStruct((META_ROWS, LANES), F32),
                   jax.ShapeDtypeStruct((steps * META_ROWS, LANES), jnp.int32)],
        scratch_shapes=[small, small, small],
        compiler_params=_params("arbitrary", "arbitrary"),
        name="moe_plan",
    )(route)


def _row_copy(src_ref, src_row, dst_ref, dst_row, sem):
    return pltpu.make_async_copy(src_ref.at[pl.ds(src_row, 1)], dst_ref.at[pl.ds(dst_row, 1)], sem)


def _for_each_run_piece(seg_ref, fn):
    def per_expert(e, carry):
        local = seg_ref[0, 0, e]
        length = seg_ref[0, 1, e]
        dst = seg_ref[0, 2, e]
        done = jnp.int32(0)
        for size in RUN_SIZES:
            has = (length & size) != 0

            @pl.when(has)
            def _(size=size, done=done):
                fn(pl.multiple_of(local + done, ROW_GROUP), pl.multiple_of(dst + done, ROW_GROUP), size)

            done = done + jnp.where(has, size, 0)
        return carry

    lax.fori_loop(0, N_EXPERTS, per_expert, 0)


def _dispatch_kernel(meta_ref, h_ref, pos_ref, seg_ref, seg_prev_ref, xs_ref, local_ref, zero_ref, sem, fill_sem,
                     *, tm, tile_rows):
    i = pl.program_id(0)
    last = pl.num_programs(0) - 1
    buf = lax.rem(i, 2)
    rows = local_ref.shape[1]
    local_pos = jnp.transpose(pos_ref[...].astype(F32))
    lp1 = local_pos[2:3].astype(jnp.int32)
    lp2 = local_pos[3:4].astype(jnp.int32)
    r = lax.broadcasted_iota(jnp.int32, (rows, tm), 0)
    place = jnp.where(r == lp1, 1.0, jnp.where(r == lp2, 1.0, 0.0)).astype(BF16)
    local_ref[buf] = _dot(place, h_ref[...])

    def run_copy(b):
        def make(local, dst, size):
            return pltpu.make_async_copy(local_ref.at[b, pl.ds(local, size)], xs_ref.at[pl.ds(dst, size)],
                                         sem.at[b])
        return make

    _for_each_run_piece(seg_ref, lambda local, dst, size: run_copy(buf)(local, dst, size).start())

    @pl.when(i > 0)
    def _():
        _for_each_run_piece(seg_prev_ref, lambda local, dst, size: run_copy(1 - buf)(local, dst, size).wait())

    @pl.when(i == last)
    def _():
        _for_each_run_piece(seg_ref, lambda local, dst, size: run_copy(buf)(local, dst, size).wait())
        zero_ref[...] = jnp.zeros_like(zero_ref)

        def per_expert(e, carry):
            off = meta_ref[0, e] * tile_rows
            lo = meta_ref[2, e]
            hi = meta_ref[1, e] * tile_rows

            def fill(g, c):
                pltpu.make_async_copy(zero_ref.at[pl.ds(0, ROW_GROUP)],
                                      xs_ref.at[pl.ds(pl.multiple_of(off + g * ROW_GROUP, ROW_GROUP), ROW_GROUP)],
                                      fill_sem).start()
                return c

            def drain(g, c):
                pltpu.make_async_copy(zero_ref.at[pl.ds(0, ROW_GROUP)], xs_ref.at[pl.ds(0, ROW_GROUP)],
                                      fill_sem).wait()
                return c

            lax.fori_loop(lo // ROW_GROUP, hi // ROW_GROUP, fill, 0)
            lax.fori_loop(lo // ROW_GROUP, hi // ROW_GROUP, drain, 0)
            return carry

        lax.fori_loop(0, N_EXPERTS, per_expert, 0)

        def spare_tile(j, carry):
            c = pltpu.make_async_copy(zero_ref, xs_ref.at[pl.ds(j * tile_rows, tile_rows)], fill_sem)
            c.start()
            c.wait()
            return carry

        n_used = meta_ref[0, N_EXPERTS - 1] + meta_ref[1, N_EXPERTS - 1]
        lax.fori_loop(n_used, xs_ref.shape[0] // tile_rows, spare_tile, 0)


def _dispatch(meta_i, h2, pos, seg3, n_rows, tm, tile_rows):
    n, d = h2.shape
    local_rows = 2 * tm + N_EXPERTS * ROW_GROUP
    seg_spec = lambda index_map: pl.BlockSpec((1, META_ROWS, LANES), index_map, memory_space=pltpu.SMEM)
    return pl.pallas_call(
        functools.partial(_dispatch_kernel, tm=tm, tile_rows=tile_rows),
        grid_spec=pltpu.PrefetchScalarGridSpec(
            num_scalar_prefetch=1,
            grid=(n // tm,),
            in_specs=[pl.BlockSpec((tm, d), lambda i, meta: (i, 0)),
                      pl.BlockSpec((tm, LANES), lambda i, meta: (i, 0)),
                      seg_spec(lambda i, meta: (i, 0, 0)),
                      seg_spec(lambda i, meta: (jnp.maximum(i - 1, 0), 0, 0))],
            out_specs=pl.BlockSpec(memory_space=pl.ANY),
            scratch_shapes=[pltpu.VMEM((2, local_rows, d), F32), pltpu.VMEM((tile_rows, d), F32),
                            pltpu.SemaphoreType.DMA((2,)), pltpu.SemaphoreType.DMA(())]),
        out_shape=jax.ShapeDtypeStruct((n_rows, d), F32),
        compiler_params=_params("arbitrary"),
        name="moe_dispatch",
    )(meta_i, h2, pos, seg3, seg3)


def _gmm_kernel(te_ref, nv_ref, x_ref, wg_ref, wu_ref, wd_ref, y_ref):
    used = pl.program_id(0) < nv_ref[0]

    @pl.when(used)
    def _():
        x = x_ref[...].astype(BF16)
        a = _dot(x, wg_ref[0])
        hid = (a * jax.nn.sigmoid(a)) * _dot(x, wu_ref[0])
        y_ref[...] = _dot(hid.astype(BF16), wd_ref[0])

    @pl.when(jnp.logical_not(used))
    def _():
        y_ref[...] = jnp.zeros_like(y_ref)


def _gmm(tile_expert, n_valid, xs, wg, wu, wd, tile_rows):
    n_rows, d = xs.shape
    de = wg.shape[2]
    rows = lambda j, te, nv: (jnp.minimum(j, nv[0] - 1), 0)
    return pl.pallas_call(
        _gmm_kernel,
        grid_spec=pltpu.PrefetchScalarGridSpec(
            num_scalar_prefetch=2,
            grid=(n_rows // tile_rows,),
            in_specs=[pl.BlockSpec((tile_rows, d), rows),
                      pl.BlockSpec((1, d, de), lambda j, te, nv: (te[j], 0, 0)),
                      pl.BlockSpec((1, d, de), lambda j, te, nv: (te[j], 0, 0)),
                      pl.BlockSpec((1, de, d), lambda j, te, nv: (te[j], 0, 0))],
            out_specs=pl.BlockSpec((tile_rows, d), lambda j, te, nv: (j, 0))),
        out_shape=jax.ShapeDtypeStruct((n_rows, d), F32),
        compiler_params=_params("arbitrary"),
        name="moe_experts",
    )(tile_expert, n_valid, xs, wg, wu, wd)


def _ple_kernel(x1_ref, p_ref, route_ref, pos_ref, pos_next_ref, ys_ref, gple_ref, wgate_ref, wproj_ref, gfin_ref,
                o_ref, y_ref, sem, *, tm):
    i = pl.program_id(0)
    buf = lax.rem(i, 2)

    def gather(pos, b):
        def start(g, carry):
            base = pl.multiple_of(g * ROW_GROUP, ROW_GROUP)
            for j in range(ROW_GROUP):
                for slot in range(2):
                    src = pos[0, 0, 2 * base + (2 * j + slot)]
                    _row_copy(ys_ref, src, y_ref.at[b, slot], base + j, sem.at[b]).start(priority=slot)
            return carry

        lax.fori_loop(0, tm // ROW_GROUP, start, 0)

    @pl.when(i == 0)
    def _():
        gather(pos_ref, 0)

    @pl.when(i + 1 < pl.num_programs(0))
    def _():
        gather(pos_next_ref, 1 - buf)

    def wait(r, carry):
        _row_copy(ys_ref, 0, y_ref.at[buf, 0], 0, sem.at[buf]).wait()
        return carry

    emb = _dot(p_ref[...].astype(BF16), wproj_ref[...])
    lax.fori_loop(0, 2 * tm, wait, 0, unroll=8)
    route = route_ref[...]
    x = x1_ref[...] + (route[:, 2:3] * y_ref[buf, 0] + route[:, 3:4] * y_ref[buf, 1])
    gate = jax.nn.sigmoid(_dot(_rms(x, gple_ref[...]).astype(BF16), wgate_ref[...]))
    o_ref[...] = _rms(x + gate * emb, gfin_ref[...])


def _ple(x1, p2d, route, pos3, ys, g_ple, w_gate, w_proj, g_final, tm):
    n, d = x1.shape
    pd = p2d.shape[1]
    steps = n // tm
    row = lambda i: (i, 0)
    const = lambda i: (0, 0)
    pos_spec = lambda index_map: pl.BlockSpec((1, 1, 2 * tm), index_map, memory_space=pltpu.SMEM)
    return pl.pallas_call(
        functools.partial(_ple_kernel, tm=tm),
        grid=(steps,),
        in_specs=[pl.BlockSpec((tm, d), row), pl.BlockSpec((tm, pd), row), pl.BlockSpec((tm, LANES), row),
                  pos_spec(lambda i: (i, 0, 0)), pos_spec(lambda i: (jnp.minimum(i + 1, steps - 1), 0, 0)),
                  pl.BlockSpec(memory_space=pl.ANY), pl.BlockSpec((1, d), const),
                  pl.BlockSpec((d, d), const), pl.BlockSpec((pd, d), const), pl.BlockSpec((1, d), const)],
        out_specs=pl.BlockSpec((tm, d), row),
        out_shape=jax.ShapeDtypeStruct((n, d), F32),
        scratch_shapes=[pltpu.VMEM((2, 2, tm, d), F32), pltpu.SemaphoreType.DMA((2,))],
        compiler_params=_params("arbitrary"),
        name="combine_ple_final",
    )(x1, p2d, route, pos3, pos3, ys, g_ple, w_gate, w_proj, g_final)


def _layer(x, p_i, layer_idx, g_mix, w_in, lambda_q1, lambda_k1, lambda_q2, lambda_k2, g_sb_out, g_df_out,
           w_out, g_ffn, w_router_group, b_router_group, w_router_expert, b_router_expert, w_expert_gate,
           w_expert_up, w_expert_down, g_ple, w_ple_gate, w_ple_proj, g_out):
    b, s, d = x.shape
    n = b * s
    tm = min(512, n)
    t = min(256, s)
    x2d = x.reshape(n, d)
    lambda_init = 0.8 - 0.6 * math.exp(-0.3 * layer_idx)

    proj = _inproj(x2d, g_mix.reshape(1, d), w_in.astype(BF16), tm)
    proj3 = proj.reshape(b, s, proj.shape[1])

    tri = (lax.broadcasted_iota(jnp.int32, (t, t), 0) > lax.broadcasted_iota(jnp.int32, (t, t), 1))
    u2 = jnp.concatenate([tri, tri], axis=0).astype(BF16)
    o_sb = _sb_attention(proj3, u2, t).reshape(n, SB_WIDTH)

    lam_rows = jnp.stack([lambda_q1, lambda_k1, lambda_q2, lambda_k2]).astype(F32)
    o_df = _df_attention(proj3, lam_rows, g_df_out.reshape(1, LANES).astype(F32), min(512, s),
                         lambda_init).reshape(n, DF_WIDTH)

    w_router = jnp.concatenate([w_router_group, w_router_expert], axis=1).astype(F32)
    w_router = jnp.pad(w_router, ((0, 0), (0, LANES - w_router.shape[1])))
    wr_hi, wr_lo = _split_bf16(w_router)
    b_router = jnp.pad(jnp.concatenate([b_router_group, b_router_expert]).astype(F32),
                       (0, LANES - N_GROUPS - N_EXPERTS)).reshape(1, LANES)
    w_out_bf = w_out.astype(BF16)
    x1, h2, route = _outproj(o_sb, o_df, x2d, g_sb_out.reshape(1, SB_WIDTH), w_out_bf[:SB_WIDTH],
                             w_out_bf[SB_WIDTH:], g_ffn.reshape(1, d),
                             jnp.concatenate([wr_hi, wr_lo, wr_hi], axis=0), b_router, tm)

    tile_rows = min(MOE_TILE_ROWS, n)
    pos, meta, seg = _plan(route, tm, tile_rows)
    run_padding = (n // tm) * N_EXPERTS * ROW_GROUP
    n_tiles = pl.cdiv(2 * n + run_padding, tile_rows) + N_EXPERTS
    meta_i = meta[:3, :N_EXPERTS].astype(jnp.int32)
    ends = meta_i[0] + meta_i[1]
    n_valid = ends[-1:]
    tile_ids = jnp.arange(n_tiles, dtype=jnp.int32)
    tile_expert = jnp.sum(tile_ids[:, None] >= ends[None, :], axis=1).astype(jnp.int32)
    tile_expert = jnp.where(tile_ids < n_valid, tile_expert, tile_expert[n_valid[0] - 1])
    pos2 = pos[:, :2]
    xs = _dispatch(meta_i, h2, pos, seg.reshape(n // tm, META_ROWS, LANES), n_tiles * tile_rows, tm, tile_rows)
    ys = _gmm(tile_expert, n_valid, xs, w_expert_gate.astype(BF16), w_expert_up.astype(BF16),
              w_expert_down.astype(BF16), tile_rows)

    tc = min(COMBINE_ROWS, n)
    out = _ple(x1, p_i.reshape(n, p_i.shape[-1]), route, pos2.reshape(n // tc, 1, 2 * tc), ys,
               g_ple.reshape(1, d), w_ple_gate.astype(BF16), w_ple_proj.astype(BF16), g_out.reshape(1, d), tc)
    return out.reshape(b, s, d)


def kernel(x, p, g_mix, w_in, lambda_q1, lambda_k1, lambda_q2, lambda_k2, g_sb_out, g_df_out, w_out, g_ffn,
           w_router_group, b_router_group, w_router_expert, b_router_expert, w_expert_gate, w_expert_up,
           w_expert_down, g_ple, w_ple_gate, w_ple_proj, g_final):
    depth = p.shape[0]
    assert depth == 1, "the final norm is fused into the single layer's last kernel"
    return _layer(x, p[0], 0, g_mix[0], w_in[0], lambda_q1[0], lambda_k1[0], lambda_q2[0], lambda_k2[0],
                  g_sb_out[0], g_df_out[0], w_out[0], g_ffn[0], w_router_group[0], b_router_group[0],
                  w_router_expert[0], b_router_expert[0], w_expert_gate[0], w_expert_up[0],
                  w_expert_down[0], g_ple[0], w_ple_gate[0], w_ple_proj[0], g_final)
```

```python
import functools
import math

import jax
import jax.numpy as jnp
from jax import lax
from jax.experimental import pallas as pl
from jax.experimental.pallas import tpu as pltpu

F32 = jnp.float32
BF16 = jnp.bfloat16

HEAD_DIM = 64
LANES = 128
N_SB_PAIRS = 4
N_DF_HEADS = 4
SB_WIDTH = 512
DF_WIDTH = 512
SCALE = HEAD_DIM ** -0.5
NORM_EPS = 1e-6
N_GROUPS = 4
EXPERTS_PER_GROUP = 4
N_EXPERTS = 16
ROUTER_LANE0 = N_GROUPS
ALIBI_SLOPES = tuple(2.0 ** (-8.0 * (h + 1) / N_DF_HEADS) for h in range(N_DF_HEADS))
SB_LOG_ZERO = -110.0
VMEM_LIMIT = 48 * 1024 * 1024
MOE_TILE_ROWS = 512
COMBINE_ROWS = 512


def _rms(x, g):
    ms = jnp.mean(x * x, axis=-1, keepdims=True)
    return x * lax.rsqrt(ms + NORM_EPS) * g


def _dot(a, b):
    return jnp.dot(a, b, preferred_element_type=F32)


def _dot_nt(a, b):
    return lax.dot_general(a, b, (((1,), (1,)), ((), ())), preferred_element_type=F32)


def _rows_dot(dot, a, b, rows):
    return jnp.concatenate([dot(a[r:r + rows], b) for r in range(0, a.shape[0], rows)], axis=0)


def _params(*sem):
    return pltpu.CompilerParams(dimension_semantics=sem, vmem_limit_bytes=VMEM_LIMIT)


def _inproj_kernel(x_ref, g_ref, w_ref, o_ref, *, tn):
    h = _rms(x_ref[...], g_ref[...]).astype(BF16)
    for j in range(o_ref.shape[1] // tn):
        o_ref[:, j * tn:(j + 1) * tn] = _dot(h, w_ref[:, j * tn:(j + 1) * tn]).astype(o_ref.dtype)


def _inproj(x2d, g, w_bf16, tm):
    n, d = x2d.shape
    width = w_bf16.shape[1]
    return pl.pallas_call(
        functools.partial(_inproj_kernel, tn=1024),
        grid=(n // tm,),
        in_specs=[pl.BlockSpec((tm, d), lambda i: (i, 0)),
                  pl.BlockSpec((1, d), lambda i: (0, 0)),
                  pl.BlockSpec((d, width), lambda i: (0, 0))],
        out_specs=pl.BlockSpec((tm, width), lambda i: (i, 0)),
        out_shape=jax.ShapeDtypeStruct((n, width), BF16),
        compiler_params=_params("parallel"),
        name="inproj",
    )(x2d, g, w_bf16)


SB_STREAMS = 2


def _sb_kernel(q_ref, k_ref, v_ref, u_ref, o_ref, acc_ref, c_ref, z_ref, *, t):
    qi = pl.program_id(2)
    lane = lax.broadcasted_iota(jnp.int32, (t, LANES), 1)
    first = lane < HEAD_DIM

    def stacked_q(p):
        q = q_ref[0, :, p * LANES:(p + 1) * LANES] * SCALE
        zero = jnp.zeros_like(q)
        return jnp.concatenate([jnp.where(first, q, zero), jnp.where(first, zero, q)], axis=0)

    q2 = [stacked_q(p) for p in range(SB_STREAMS)]

    def logits(p, kb):
        k = k_ref[0, pl.ds(pl.multiple_of(kb * t, t), t), p * LANES:(p + 1) * LANES]
        return _rows_dot(_dot_nt, q2[p], k, t)

    def consume(p, z, kb, strict_mask):
        v = v_ref[0, pl.ds(pl.multiple_of(kb * t, t), t), p * LANES:(p + 1) * LANES]
        soft = jnp.log(1.0 + jnp.exp(-jnp.abs(z)))
        log_beta = jnp.minimum(z, 0.0) - soft
        log_keep = log_beta - z
        if strict_mask is not None:
            log_keep = jnp.where(strict_mask, log_keep, 0.0)
        hi = log_keep.astype(BF16)
        lo = (log_keep - hi.astype(F32)).astype(BF16)
        rev = _rows_dot(_dot, jnp.concatenate([hi, lo], axis=1), u_ref[...], t)
        c = c_ref[p]
        w = jnp.exp(log_beta + rev + jnp.concatenate([c] * (t // LANES), axis=1))
        if strict_mask is not None:
            w = jnp.where(strict_mask, w, 0.0)
        w = w.astype(BF16)
        vz = jnp.zeros_like(v)
        v2 = jnp.concatenate([jnp.where(first, v, vz), jnp.where(first, vz, v)], axis=0)
        acc_ref[p] += _dot(jnp.concatenate([w[:t], w[t:]], axis=1), v2)
        c_new = c + jnp.sum(log_keep, axis=-1, keepdims=True)
        c_ref[p] = c_new
        return jnp.max(c_new)

    def step(kb, strict_mask):
        kb_next = jnp.maximum(kb - 1, 0)
        worst = None
        for p in range(SB_STREAMS):
            z = z_ref[p]
            z_ref[p] = logits(p, kb_next)
            done = consume(p, z, kb, strict_mask)
            worst = done if worst is None else jnp.maximum(worst, done)
        return worst

    acc_ref[...] = jnp.zeros_like(acc_ref)
    c_ref[...] = jnp.zeros_like(c_ref)
    for p in range(SB_STREAMS):
        z_ref[p] = logits(p, qi)
    row = lax.broadcasted_iota(jnp.int32, (2 * t, t), 0)
    col = lax.broadcasted_iota(jnp.int32, (2 * t, t), 1)
    cmax = step(qi, col < jnp.where(row >= t, row - t, row))

    def cond(carry):
        kb, cm = carry
        return jnp.logical_and(kb >= 0, cm > SB_LOG_ZERO)

    def body(carry):
        kb, _ = carry
        return kb - 1, step(kb, None)

    lax.while_loop(cond, body, (qi - 1, cmax))
    for p in range(SB_STREAMS):
        o_ref[0, :, p * LANES:(p + 1) * LANES] = acc_ref[p]


def _sb_attention(proj3, u2, t):
    b, s, _ = proj3.shape
    groups = N_SB_PAIRS // SB_STREAMS
    width = SB_STREAMS * LANES
    return pl.pallas_call(
        functools.partial(_sb_kernel, t=t),
        grid=(b, groups, s // t),
        in_specs=[pl.BlockSpec((1, t, width), lambda bi, g, qi: (bi, qi, g)),
                  pl.BlockSpec((1, s, width), lambda bi, g, qi: (bi, 0, groups + g)),
                  pl.BlockSpec((1, s, width), lambda bi, g, qi: (bi, 0, 2 * groups + g)),
                  pl.BlockSpec((2 * t, t), lambda bi, g, qi: (0, 0))],
        out_specs=pl.BlockSpec((1, t, width), lambda bi, g, qi: (bi, qi, g)),
        out_shape=jax.ShapeDtypeStruct((b, s, SB_WIDTH), F32),
        scratch_shapes=[pltpu.VMEM((SB_STREAMS, t, LANES), F32), pltpu.VMEM((SB_STREAMS, 2 * t, LANES), F32),
                        pltpu.VMEM((SB_STREAMS, 2 * t, t), F32)],
        compiler_params=_params("parallel", "parallel", "arbitrary"),
        name="sb_attention",
    )(proj3, proj3, proj3, u2)


def _df_kernel(q_ref, k_ref, v_ref, lam_ref, g_ref, o_ref, acc_ref, m_ref, z_ref, *, t, lambda_init):
    h = pl.program_id(1)
    qi = pl.program_id(2)
    slope = jnp.float32(ALIBI_SLOPES[-1])
    for idx in range(N_DF_HEADS - 2, -1, -1):
        slope = jnp.where(h == idx, jnp.float32(ALIBI_SLOPES[idx]), slope)
    lane = lax.broadcasted_iota(jnp.int32, (t, LANES), 1)
    row = lax.broadcasted_iota(jnp.int32, (t, LANES), 0)
    first = lane < HEAD_DIM
    q = q_ref[0] * SCALE
    zero = jnp.zeros_like(q)
    bias_on = jnp.where(lane < 2, 1.0, 0.0).astype(BF16)
    q2 = jnp.concatenate([jnp.concatenate([jnp.where(first, q, zero), bias_on], axis=1),
                          jnp.concatenate([jnp.where(first, zero, q), bias_on], axis=1)], axis=0)
    key_lo = jnp.where(lane == 0, ((row >> 8) << 8).astype(F32),
                       jnp.where(lane == 1, (row & 255).astype(F32), 0.0)) * slope
    lane0 = lane == 0
    ones_v = jnp.ones((t, LANES), BF16)

    def logits(kb):
        k = k_ref[0, pl.ds(pl.multiple_of(kb * t, t), t), :]
        offset = slope * ((kb - qi) * t).astype(F32)
        k_bias = (key_lo + jnp.where(lane0, offset, 0.0)).astype(BF16)
        return _dot_nt(q2, jnp.concatenate([k, k_bias], axis=1))

    def accumulate(z, kb):
        v = v_ref[0, pl.ds(pl.multiple_of(kb * t, t), t), :]
        m_old = m_ref[...]
        m_new = jnp.maximum(m_old, jnp.max(z, axis=-1, keepdims=True))
        alpha = jnp.exp(m_old - m_new)
        p = jnp.exp(z - jnp.concatenate([m_new] * (t // LANES), axis=1)).astype(BF16)
        pv = _dot(p, jnp.concatenate([v, ones_v], axis=1))
        acc_ref[...] = jnp.concatenate([alpha, alpha], axis=1) * acc_ref[...] + pv
        m_ref[...] = m_new

    acc_ref[...] = jnp.zeros_like(acc_ref)
    m_ref[...] = jnp.full_like(m_ref, -jnp.inf)
    r2 = lax.broadcasted_iota(jnp.int32, (2 * t, t), 0)
    c2 = lax.broadcasted_iota(jnp.int32, (2 * t, t), 1)
    z_ref[...] = jnp.where(c2 <= jnp.where(r2 >= t, r2 - t, r2), logits(qi), -jnp.inf)

    def body(j, carry):
        z = z_ref[...]
        z_ref[...] = logits(j)
        accumulate(z, jnp.where(j == 0, qi, j - 1))
        return carry

    lax.fori_loop(0, qi, body, 0)
    accumulate(z_ref[...], jnp.where(qi == 0, qi, qi - 1))

    lam_vec = lam_ref[...]
    lam = (jnp.exp(jnp.sum(lam_vec[0:1] * lam_vec[1:2], axis=-1, keepdims=True))
           - jnp.exp(jnp.sum(lam_vec[2:3] * lam_vec[3:4], axis=-1, keepdims=True))
           + lambda_init)
    acc = acc_ref[...]
    ratio = acc[:, :LANES] / acc[:, LANES:]
    o = ratio[:t] - lam * ratio[t:]
    o_ref[0] = (_rms(o, g_ref[...]) * (1.0 - lambda_init)).astype(o_ref.dtype)


def _df_attention(proj3, lam_rows, g_df, t, lambda_init):
    b, s, _ = proj3.shape
    col0 = 3 * N_SB_PAIRS
    return pl.pallas_call(
        functools.partial(_df_kernel, t=t, lambda_init=lambda_init),
        grid=(b, N_DF_HEADS, s // t),
        in_specs=[pl.BlockSpec((1, t, LANES), lambda bi, h, qi: (bi, qi, col0 + h)),
                  pl.BlockSpec((1, s, LANES), lambda bi, h, qi: (bi, 0, col0 + N_DF_HEADS + h)),
                  pl.BlockSpec((1, s, LANES), lambda bi, h, qi: (bi, 0, col0 + 2 * N_DF_HEADS + h)),
                  pl.BlockSpec((4, HEAD_DIM), lambda bi, h, qi: (0, 0)),
                  pl.BlockSpec((1, LANES), lambda bi, h, qi: (0, 0))],
        out_specs=pl.BlockSpec((1, t, LANES), lambda bi, h, qi: (bi, qi, h)),
        out_shape=jax.ShapeDtypeStruct((b, s, DF_WIDTH), BF16),
        scratch_shapes=[pltpu.VMEM((2 * t, 2 * LANES), F32), pltpu.VMEM((2 * t, LANES), F32),
                        pltpu.VMEM((2 * t, t), F32)],
        compiler_params=_params("parallel", "parallel", "arbitrary"),
        name="df_attention",
    )(proj3, proj3, proj3, lam_rows, g_df)


def _split_bf16(x):
    hi = x.astype(BF16)
    return hi, (x - hi.astype(F32)).astype(BF16)


def _route(logits):
    lane = lax.broadcasted_iota(jnp.int32, logits.shape, 1)
    neg = -jnp.inf
    big = jnp.int32(LANES)
    gl = jnp.where(lane < N_GROUPS, logits, neg)
    gmax = jnp.max(gl, axis=-1, keepdims=True)
    g_idx = jnp.min(jnp.where(gl == gmax, lane, big), axis=-1, keepdims=True)
    g_w = 1.0 / jnp.sum(jnp.exp(gl - gmax), axis=-1, keepdims=True)
    e_lane = lane - ROUTER_LANE0
    in_group = jnp.logical_and(jnp.logical_and(e_lane >= 0, e_lane < N_EXPERTS),
                               (e_lane // EXPERTS_PER_GROUP) == g_idx)
    v1 = jnp.where(in_group, logits, neg)
    t1 = jnp.max(v1, axis=-1, keepdims=True)
    i1 = jnp.min(jnp.where(v1 == t1, lane, big), axis=-1, keepdims=True)
    v2 = jnp.where(lane == i1, neg, v1)
    t2 = jnp.max(v2, axis=-1, keepdims=True)
    i2 = jnp.min(jnp.where(v2 == t2, lane, big), axis=-1, keepdims=True)
    e2 = jnp.exp(t2 - t1)
    w1 = g_w / (1.0 + e2)
    w2 = w1 * e2
    id1 = (i1 - ROUTER_LANE0).astype(F32)
    id2 = (i2 - ROUTER_LANE0).astype(F32)
    return jnp.where(lane == 0, id1, jnp.where(lane == 1, id2,
                                               jnp.where(lane == 2, w1, jnp.where(lane == 3, w2, 0.0))))


def _outproj_kernel(osb_ref, odf_ref, x_ref, gsb_ref, wsb_ref, wdf_ref, gffn_ref, wr_ref, br_ref,
                    x1_ref, h2_ref, route_ref):
    a_sb = _rms(osb_ref[...], gsb_ref[...]).astype(BF16)
    x1 = x_ref[...] + _dot(a_sb, wsb_ref[...]) + _dot(odf_ref[...], wdf_ref[...])
    x1_ref[...] = x1
    h2 = _rms(x1, gffn_ref[...])
    h2_ref[...] = h2.astype(BF16)
    hi, lo = _split_bf16(h2)
    logits = _rows_dot(_dot, jnp.concatenate([hi, hi, lo], axis=1), wr_ref[...], hi.shape[0] // 2) + br_ref[...]
    route_ref[...] = _route(logits)


def _outproj(o_sb, o_df, x2d, g_sb, w_sb, w_df, g_ffn, w_router3, b_router, tm):
    n, d = x2d.shape
    row = lambda i: (i, 0)
    const = lambda i: (0, 0)
    return pl.pallas_call(
        _outproj_kernel,
        grid=(n // tm,),
        in_specs=[pl.BlockSpec((tm, SB_WIDTH), row), pl.BlockSpec((tm, DF_WIDTH), row),
                  pl.BlockSpec((tm, d), row), pl.BlockSpec((1, SB_WIDTH), const),
                  pl.BlockSpec((SB_WIDTH, d), const), pl.BlockSpec((DF_WIDTH, d), const),
                  pl.BlockSpec((1, d), const), pl.BlockSpec((3 * d, LANES), const),
                  pl.BlockSpec((1, LANES), const)],
        out_specs=[pl.BlockSpec((tm, d), row), pl.BlockSpec((tm, d), row),
                   pl.BlockSpec((tm, LANES), row)],
        out_shape=[jax.ShapeDtypeStruct((n, d), F32), jax.ShapeDtypeStruct((n, d), BF16),
                   jax.ShapeDtypeStruct((n, LANES), F32)],
        compiler_params=_params("parallel"),
        name="outproj_router",
    )(o_sb, o_df, x2d, g_sb, w_sb, w_df, g_ffn, w_router3, b_router)


META_ROWS = 8
ROW_GROUP = 8
RUN_SIZES = (512, 256, 128, 64, 32, 16, 8)


def _lane_prefix(x_groups):
    r = lax.broadcasted_iota(jnp.int32, (LANES, LANES), 0)
    c = lax.broadcasted_iota(jnp.int32, (LANES, LANES), 1)
    return _dot(x_groups.astype(BF16), jnp.where(r < c, 1.0, 0.0).astype(BF16))


def _plan_kernel(route_ref, pos_ref, meta_ref, seg_ref, cnt_ref, run_ref, off_ref, *, tm, tile_rows):
    phase = pl.program_id(0)
    i = pl.program_id(1)
    route = route_ref[...]
    lane = lax.broadcasted_iota(jnp.int32, (tm, LANES), 1)
    lane_f = lane.astype(F32)
    sel1 = lane_f == route[:, 0:1]
    sel2 = lane_f == route[:, 1:2]
    onehot = jnp.where(sel1, 1.0, jnp.where(sel2, 1.0, 0.0))
    run_groups = jnp.ceil(jnp.sum(onehot, axis=0, keepdims=True) * (1.0 / ROW_GROUP))
    run_rows = run_groups * ROW_GROUP

    @pl.when(jnp.logical_and(phase == 0, i == 0))
    def _():
        cnt_ref[...] = jnp.zeros_like(cnt_ref)

    @pl.when(phase == 0)
    def _():
        cnt_ref[...] += run_rows

    @pl.when(jnp.logical_and(phase == 1, i == 0))
    def _():
        cnt = cnt_ref[...]
        n_tiles = jnp.ceil(cnt * (1.0 / tile_rows))
        tile_off = _lane_prefix(n_tiles)
        off_ref[...] = tile_off * tile_rows
        run_ref[...] = jnp.zeros_like(run_ref)
        mrow = lax.broadcasted_iota(jnp.int32, (META_ROWS, LANES), 0)
        meta_ref[...] = jnp.where(mrow == 0, tile_off, jnp.where(mrow == 1, n_tiles,
                                                                 jnp.where(mrow == 2, cnt, 0.0)))

    @pl.when(phase == 1)
    def _():
        r = lax.broadcasted_iota(jnp.int32, (tm, tm), 0)
        c = lax.broadcasted_iota(jnp.int32, (tm, tm), 1)
        earlier = _dot(jnp.where(c < r, 1.0, 0.0).astype(BF16), onehot.astype(BF16))
        local_start = _lane_prefix(jnp.broadcast_to(run_groups, (META_ROWS, LANES))) * ROW_GROUP
        global_start = run_ref[...] + off_ref[...]
        base_g = earlier + global_start[0:1]
        base_l = earlier + local_start[0:1]

        def pick(sel, base):
            return jnp.sum(jnp.where(sel, base, 0.0), axis=-1, keepdims=True)

        pos_ref[...] = jnp.where(
            lane == 0, pick(sel1, base_g), jnp.where(
                lane == 1, pick(sel2, base_g), jnp.where(
                    lane == 2, pick(sel1, base_l), jnp.where(
                        lane == 3, pick(sel2, base_l), 0.0)))).astype(jnp.int32)
        mrow = lax.broadcasted_iota(jnp.int32, (META_ROWS, LANES), 0)
        seg_ref[...] = jnp.where(mrow == 0, local_start, jnp.where(
            mrow == 1, run_rows, jnp.where(mrow == 2, global_start, 0.0))).astype(jnp.int32)
        run_ref[...] += run_rows


def _plan(route, tm, tile_rows):
    n = route.shape[0]
    steps = n // tm
    assert (n + (ROW_GROUP - 1) * steps) // tile_rows + 1 <= 256, "per-expert tile counts must stay exact in bf16"
    assert (2 * tm) // ROW_GROUP + N_EXPERTS <= 256, "per-tile run sizes must stay exact in bf16"
    small = pltpu.VMEM((META_ROWS, LANES), F32)
    return pl.pallas_call(
        functools.partial(_plan_kernel, tm=tm, tile_rows=tile_rows),
        grid=(2, steps),
        in_specs=[pl.BlockSpec((tm, LANES), lambda ph, i: (i, 0))],
        out_specs=[pl.BlockSpec((tm, LANES), lambda ph, i: (i * ph, 0)),
                   pl.BlockSpec((META_ROWS, LANES), lambda ph, i: (0, 0)),
                   pl.BlockSpec((META_ROWS, LANES), lambda ph, i: (i * ph, 0))],
        out_shape=[jax.ShapeDtypeStruct((n, LANES), jnp.int32),
                   jax.ShapeDtypeStruct((META_ROWS, LANES), F32),
                   jax.ShapeDtypeStruct((steps * META_ROWS, LANES), jnp.int32)],
        scratch_shapes=[small, small, small],
        compiler_params=_params("arbitrary", "arbitrary"),
        name="moe_plan",
    )(route)


def _row_copy(src_ref, src_row, dst_ref, dst_row, sem):
    return pltpu.make_async_copy(src_ref.at[pl.ds(src_row, 1)], dst_ref.at[pl.ds(dst_row, 1)], sem)


def _for_each_run_piece(seg_ref, fn):
    def per_expert(e, carry):
        local = seg_ref[0, 0, e]
        length = seg_ref[0, 1, e]
        dst = seg_ref[0, 2, e]
        done = jnp.int32(0)
        for size in RUN_SIZES:
            has = (length & size) != 0

            @pl.when(has)
            def _(size=size, done=done):
                fn(pl.multiple_of(local + done, ROW_GROUP), pl.multiple_of(dst + done, ROW_GROUP), size)

            done = done + jnp.where(has, size, 0)
        return carry

    lax.fori_loop(0, N_EXPERTS, per_expert, 0)


def _dispatch_kernel(meta_ref, h_ref, pos_ref, seg_ref, seg_prev_ref, xs_ref, local_ref, zero_ref, sem, fill_sem,
                     *, tm, tile_rows):
    i = pl.program_id(0)
    last = pl.num_programs(0) - 1
    buf = lax.rem(i, 2)
    rows = local_ref.shape[1]
    local_pos = jnp.transpose(pos_ref[...].astype(F32))
    lp1 = local_pos[2:3].astype(jnp.int32)
    lp2 = local_pos[3:4].astype(jnp.int32)
    r = lax.broadcasted_iota(jnp.int32, (rows, tm), 0)
    place = jnp.where(r == lp1, 1.0, jnp.where(r == lp2, 1.0, 0.0)).astype(BF16)
    local_ref[buf] = _dot(place, h_ref[...])

    def run_copy(b):
        def make(local, dst, size):
            return pltpu.make_async_copy(local_ref.at[b, pl.ds(local, size)], xs_ref.at[pl.ds(dst, size)],
                                         sem.at[b])
        return make

    _for_each_run_piece(seg_ref, lambda local, dst, size: run_copy(buf)(local, dst, size).start())

    @pl.when(i > 0)
    def _():
        _for_each_run_piece(seg_prev_ref, lambda local, dst, size: run_copy(1 - buf)(local, dst, size).wait())

    @pl.when(i == last)
    def _():
        _for_each_run_piece(seg_ref, lambda local, dst, size: run_copy(buf)(local, dst, size).wait())
        zero_ref[...] = jnp.zeros_like(zero_ref)

        def per_expert(e, carry):
            off = meta_ref[0, e] * tile_rows
            lo = meta_ref[2, e]
            hi = meta_ref[1, e] * tile_rows

            def fill(g, c):
                pltpu.make_async_copy(zero_ref.at[pl.ds(0, ROW_GROUP)],
                                      xs_ref.at[pl.ds(pl.multiple_of(off + g * ROW_GROUP, ROW_GROUP), ROW_GROUP)],
                                      fill_sem).start()
                return c

            def drain(g, c):
                pltpu.make_async_copy(zero_ref.at[pl.ds(0, ROW_GROUP)], xs_ref.at[pl.ds(0, ROW_GROUP)],
                                      fill_sem).wait()
                return c

            lax.fori_loop(lo // ROW_GROUP, hi // ROW_GROUP, fill, 0)
            lax.fori_loop(lo // ROW_GROUP, hi // ROW_GROUP, drain, 0)
            return carry

        lax.fori_loop(0, N_EXPERTS, per_expert, 0)

        def spare_tile(j, carry):
            c = pltpu.make_async_copy(zero_ref, xs_ref.at[pl.ds(j * tile_rows, tile_rows)], fill_sem)
            c.start()
            c.wait()
            return carry

        n_used = meta_ref[0, N_EXPERTS - 1] + meta_ref[1, N_EXPERTS - 1]
        lax.fori_loop(n_used, xs_ref.shape[0] // tile_rows, spare_tile, 0)


def _dispatch(meta_i, h2, pos, seg3, n_rows, tm, tile_rows):
    n, d = h2.shape
    local_rows = 2 * tm + N_EXPERTS * ROW_GROUP
    seg_spec = lambda index_map: pl.BlockSpec((1, META_ROWS, LANES), index_map, memory_space=pltpu.SMEM)
    return pl.pallas_call(
        functools.partial(_dispatch_kernel, tm=tm, tile_rows=tile_rows),
        grid_spec=pltpu.PrefetchScalarGridSpec(
            num_scalar_prefetch=1,
            grid=(n // tm,),
            in_specs=[pl.BlockSpec((tm, d), lambda i, meta: (i, 0)),
                      pl.BlockSpec((tm, LANES), lambda i, meta: (i, 0)),
                      seg_spec(lambda i, meta: (i, 0, 0)),
                      seg_spec(lambda i, meta: (jnp.maximum(i - 1, 0), 0, 0))],
            out_specs=pl.BlockSpec(memory_space=pl.ANY),
            scratch_shapes=[pltpu.VMEM((2, local_rows, d), F32), pltpu.VMEM((tile_rows, d), F32),
                            pltpu.SemaphoreType.DMA((2,)), pltpu.SemaphoreType.DMA(())]),
        out_shape=jax.ShapeDtypeStruct((n_rows, d), F32),
        compiler_params=_params("arbitrary"),
        name="moe_dispatch",
    )(meta_i, h2, pos, seg3, seg3)


def _gmm_kernel(te_ref, nv_ref, x_ref, wg_ref, wu_ref, wd_ref, y_ref, wg_bf, wu_bf, wd_bf):
    j = pl.program_id(0)
    used = j < nv_ref[0]
    new_expert = jnp.logical_or(j == 0, te_ref[j] != te_ref[jnp.maximum(j - 1, 0)])

    @pl.when(jnp.logical_and(used, new_expert))
    def _():
        wg_bf[...] = wg_ref[0].astype(BF16)
        wu_bf[...] = wu_ref[0].astype(BF16)
        wd_bf[...] = wd_ref[0].astype(BF16)

    @pl.when(used)
    def _():
        x = x_ref[...].astype(BF16)
        a = _dot(x, wg_bf[...])
        hid = (a * jax.nn.sigmoid(a)) * _dot(x, wu_bf[...])
        y_ref[...] = _dot(hid.astype(BF16), wd_bf[...])

    @pl.when(jnp.logical_not(used))
    def _():
        y_ref[...] = jnp.zeros_like(y_ref)


def _gmm(tile_expert, n_valid, xs, wg, wu, wd, tile_rows):
    n_rows, d = xs.shape
    de = wg.shape[2]
    rows = lambda j, te, nv: (jnp.minimum(j, nv[0] - 1), 0)
    return pl.pallas_call(
        _gmm_kernel,
        grid_spec=pltpu.PrefetchScalarGridSpec(
            num_scalar_prefetch=2,
            grid=(n_rows // tile_rows,),
            in_specs=[pl.BlockSpec((tile_rows, d), rows),
                      pl.BlockSpec((1, d, de), lambda j, te, nv: (te[j], 0, 0)),
                      pl.BlockSpec((1, d, de), lambda j, te, nv: (te[j], 0, 0)),
                      pl.BlockSpec((1, de, d), lambda j, te, nv: (te[j], 0, 0))],
            out_specs=pl.BlockSpec((tile_rows, d), lambda j, te, nv: (j, 0)),
            scratch_shapes=[pltpu.VMEM((d, de), BF16), pltpu.VMEM((d, de), BF16), pltpu.VMEM((de, d), BF16)]),
        out_shape=jax.ShapeDtypeStruct((n_rows, d), F32),
        compiler_params=_params("arbitrary"),
        name="moe_experts",
    )(tile_expert, n_valid, xs, wg, wu, wd)


def _ple_kernel(x1_ref, p_ref, route_ref, pos_ref, pos_next_ref, ys_ref, gple_ref, wgate_ref, wproj_ref, gfin_ref,
                o_ref, y_ref, sem, *, tm):
    i = pl.program_id(0)
    buf = lax.rem(i, 2)

    def gather(pos, b):
        def start(g, carry):
            base = pl.multiple_of(g * ROW_GROUP, ROW_GROUP)
            for j in range(ROW_GROUP):
                for slot in range(2):
                    src = pos[0, 0, 2 * base + (2 * j + slot)]
                    _row_copy(ys_ref, src, y_ref.at[b, slot], base + j, sem.at[b]).start(priority=slot)
            return carry

        lax.fori_loop(0, tm // ROW_GROUP, start, 0)

    @pl.when(i == 0)
    def _():
        gather(pos_ref, 0)

    @pl.when(i + 1 < pl.num_programs(0))
    def _():
        gather(pos_next_ref, 1 - buf)

    def wait(r, carry):
        _row_copy(ys_ref, 0, y_ref.at[buf, 0], 0, sem.at[buf]).wait()
        return carry

    emb = _dot(p_ref[...].astype(BF16), wproj_ref[...])
    lax.fori_loop(0, 2 * tm, wait, 0, unroll=8)
    route = route_ref[...]
    x = x1_ref[...] + (route[:, 2:3] * y_ref[buf, 0] + route[:, 3:4] * y_ref[buf, 1])
    gate = jax.nn.sigmoid(_dot(_rms(x, gple_ref[...]).astype(BF16), wgate_ref[...]))
    o_ref[...] = _rms(x + gate * emb, gfin_ref[...])


def _ple(x1, p2d, route, pos3, ys, g_ple, w_gate, w_proj, g_final, tm):
    n, d = x1.shape
    pd = p2d.shape[1]
    steps = n // tm
    row = lambda i: (i, 0)
    const = lambda i: (0, 0)
    pos_spec = lambda index_map: pl.BlockSpec((1, 1, 2 * tm), index_map, memory_space=pltpu.SMEM)
    return pl.pallas_call(
        functools.partial(_ple_kernel, tm=tm),
        grid=(steps,),
        in_specs=[pl.BlockSpec((tm, d), row), pl.BlockSpec((tm, pd), row), pl.BlockSpec((tm, LANES), row),
                  pos_spec(lambda i: (i, 0, 0)), pos_spec(lambda i: (jnp.minimum(i + 1, steps - 1), 0, 0)),
                  pl.BlockSpec(memory_space=pl.ANY), pl.BlockSpec((1, d), const),
                  pl.BlockSpec((d, d), const), pl.BlockSpec((pd, d), const), pl.BlockSpec((1, d), const)],
        out_specs=pl.BlockSpec((tm, d), row),
        out_shape=jax.ShapeDtypeStruct((n, d), F32),
        scratch_shapes=[pltpu.VMEM((2, 2, tm, d), F32), pltpu.SemaphoreType.DMA((2,))],
        compiler_params=_params("arbitrary"),
        name="combine_ple_final",
    )(x1, p2d, route, pos3, pos3, ys, g_ple, w_gate, w_proj, g_final)


def _layer(x, p_i, layer_idx, g_mix, w_in, lambda_q1, lambda_k1, lambda_q2, lambda_k2, g_sb_out, g_df_out,
           w_out, g_ffn, w_router_group, b_router_group, w_router_expert, b_router_expert, w_expert_gate,
           w_expert_up, w_expert_down, g_ple, w_ple_gate, w_ple_proj, g_out):
    b, s, d = x.shape
    n = b * s
    tm = min(512, n)
    t = min(256, s)
    x2d = x.reshape(n, d)
    lambda_init = 0.8 - 0.6 * math.exp(-0.3 * layer_idx)

    proj = _inproj(x2d, g_mix.reshape(1, d), w_in.astype(BF16), tm)
    proj3 = proj.reshape(b, s, proj.shape[1])

    tri = (lax.broadcasted_iota(jnp.int32, (t, t), 0) > lax.broadcasted_iota(jnp.int32, (t, t), 1))
    u2 = jnp.concatenate([tri, tri], axis=0).astype(BF16)
    o_sb = _sb_attention(proj3, u2, t).reshape(n, SB_WIDTH)

    lam_rows = jnp.stack([lambda_q1, lambda_k1, lambda_q2, lambda_k2]).astype(F32)
    o_df = _df_attention(proj3, lam_rows, g_df_out.reshape(1, LANES).astype(F32), min(512, s),
                         lambda_init).reshape(n, DF_WIDTH)

    w_router = jnp.concatenate([w_router_group, w_router_expert], axis=1).astype(F32)
    w_router = jnp.pad(w_router, ((0, 0), (0, LANES - w_router.shape[1])))
    wr_hi, wr_lo = _split_bf16(w_router)
    b_router = jnp.pad(jnp.concatenate([b_router_group, b_router_expert]).astype(F32),
                       (0, LANES - N_GROUPS - N_EXPERTS)).reshape(1, LANES)
    w_out_bf = w_out.astype(BF16)
    x1, h2, route = _outproj(o_sb, o_df, x2d, g_sb_out.reshape(1, SB_WIDTH), w_out_bf[:SB_WIDTH],
                             w_out_bf[SB_WIDTH:], g_ffn.reshape(1, d),
                             jnp.concatenate([wr_hi, wr_lo, wr_hi], axis=0), b_router, tm)

    tile_rows = min(MOE_TILE_ROWS, n)
    pos, meta, seg = _plan(route, tm, tile_rows)
    run_padding = (n // tm) * N_EXPERTS * ROW_GROUP
    n_tiles = pl.cdiv(2 * n + run_padding, tile_rows) + N_EXPERTS
    meta_i = meta[:3, :N_EXPERTS].astype(jnp.int32)
    ends = meta_i[0] + meta_i[1]
    n_valid = ends[-1:]
    tile_ids = jnp.arange(n_tiles, dtype=jnp.int32)
    tile_expert = jnp.sum(tile_ids[:, None] >= ends[None, :], axis=1).astype(jnp.int32)
    tile_expert = jnp.where(tile_ids < n_valid, tile_expert, tile_expert[n_valid[0] - 1])
    pos2 = pos[:, :2]
    xs = _dispatch(meta_i, h2, pos, seg.reshape(n // tm, META_ROWS, LANES), n_tiles * tile_rows, tm, tile_rows)
    ys = _gmm(tile_expert, n_valid, xs, w_expert_gate, w_expert_up, w_expert_down, tile_rows)

    tc = min(COMBINE_ROWS, n)
    out = _ple(x1, p_i.reshape(n, p_i.shape[-1]), route, pos2.reshape(n // tc, 1, 2 * tc), ys,
               g_ple.reshape(1, d), w_ple_gate.astype(BF16), w_ple_proj.astype(BF16), g_out.reshape(1, d), tc)
    return out.reshape(b, s, d)


def kernel(x, p, g_mix, w_in, lambda_q1, lambda_k1, lambda_q2, lambda_k2, g_sb_out, g_df_out, w_out, g_ffn,
           w_router_group, b_router_group, w_router_expert, b_router_expert, w_expert_gate, w_expert_up,
           w_expert_down, g_ple, w_ple_gate, w_ple_proj, g_final):
    depth = p.shape[0]
    assert depth == 1, "the final norm is fused into the single layer's last kernel"
    return _layer(x, p[0], 0, g_mix[0], w_in[0], lambda_q1[0], lambda_k1[0], lambda_q2[0], lambda_k2[0],
                  g_sb_out[0], g_df_out[0], w_out[0], g_ffn[0], w_router_group[0], b_router_group[0],
                  w_router_expert[0], b_router_expert[0], w_expert_gate[0], w_expert_up[0],
                  w_expert_down[0], g_ple[0], w_ple_gate[0], w_ple_proj[0], g_final)
```

```python
import functools
import math

import jax
import jax.numpy as jnp
from jax import lax
from jax.experimental import pallas as pl
from jax.experimental.pallas import tpu as pltpu

F32 = jnp.float32
BF16 = jnp.bfloat16

HEAD_DIM = 64
LANES = 128
N_SB_PAIRS = 4
N_DF_HEADS = 4
SB_WIDTH = 512
DF_WIDTH = 512
SCALE = HEAD_DIM ** -0.5
NORM_EPS = 1e-6
N_GROUPS = 4
EXPERTS_PER_GROUP = 4
N_EXPERTS = 16
ROUTER_LANE0 = N_GROUPS
ALIBI_SLOPES = tuple(2.0 ** (-8.0 * (h + 1) / N_DF_HEADS) for h in range(N_DF_HEADS))
SB_LOG_ZERO = -110.0
VMEM_LIMIT = 48 * 1024 * 1024
MOE_TILE_ROWS = 512
META_ROWS = 8
ROW_GROUP = 8


def _rms(x, g):
    ms = jnp.mean(x * x, axis=-1, keepdims=True)
    return x * lax.rsqrt(ms + NORM_EPS) * g


def _dot(a, b):
    return jnp.dot(a, b, preferred_element_type=F32)


def _dot_nt(a, b):
    return lax.dot_general(a, b, (((1,), (1,)), ((), ())), preferred_element_type=F32)


def _rows_dot(dot, a, b, rows):
    return jnp.concatenate([dot(a[r:r + rows], b) for r in range(0, a.shape[0], rows)], axis=0)


def _params(*sem):
    return pltpu.CompilerParams(dimension_semantics=sem, vmem_limit_bytes=VMEM_LIMIT)


def _inproj_kernel(x_ref, g_ref, w_ref, o_ref, *, tn):
    h = _rms(x_ref[...], g_ref[...]).astype(BF16)
    for j in range(o_ref.shape[1] // tn):
        o_ref[:, j * tn:(j + 1) * tn] = _dot(h, w_ref[:, j * tn:(j + 1) * tn]).astype(o_ref.dtype)


def _inproj(x2d, g, w_bf16, tm):
    n, d = x2d.shape
    width = w_bf16.shape[1]
    return pl.pallas_call(
        functools.partial(_inproj_kernel, tn=1024),
        grid=(n // tm,),
        in_specs=[pl.BlockSpec((tm, d), lambda i: (i, 0)),
                  pl.BlockSpec((1, d), lambda i: (0, 0)),
                  pl.BlockSpec((d, width), lambda i: (0, 0))],
        out_specs=pl.BlockSpec((tm, width), lambda i: (i, 0)),
        out_shape=jax.ShapeDtypeStruct((n, width), BF16),
        compiler_params=_params("parallel"),
        name="inproj",
    )(x2d, g, w_bf16)


SB_STREAMS = 2


def _sb_kernel(q_ref, k_ref, v_ref, u_ref, o_ref, acc_ref, c_ref, z_ref, *, t):
    qi = pl.program_id(2)
    lane = lax.broadcasted_iota(jnp.int32, (t, LANES), 1)
    first = lane < HEAD_DIM

    def stacked_q(p):
        q = q_ref[0, :, p * LANES:(p + 1) * LANES] * SCALE
        zero = jnp.zeros_like(q)
        return jnp.concatenate([jnp.where(first, q, zero), jnp.where(first, zero, q)], axis=0)

    q2 = [stacked_q(p) for p in range(SB_STREAMS)]

    def logits(p, kb):
        k = k_ref[0, pl.ds(pl.multiple_of(kb * t, t), t), p * LANES:(p + 1) * LANES]
        return _rows_dot(_dot_nt, q2[p], k, t)

    def consume(p, z, kb, strict_mask):
        v = v_ref[0, pl.ds(pl.multiple_of(kb * t, t), t), p * LANES:(p + 1) * LANES]
        soft = jnp.log(1.0 + jnp.exp(-jnp.abs(z)))
        log_beta = jnp.minimum(z, 0.0) - soft
        log_keep = log_beta - z
        if strict_mask is not None:
            log_keep = jnp.where(strict_mask, log_keep, 0.0)
        hi = log_keep.astype(BF16)
        lo = (log_keep - hi.astype(F32)).astype(BF16)
        rev = _rows_dot(_dot, jnp.concatenate([hi, lo], axis=1), u_ref[...], t)
        c = c_ref[p]
        w = jnp.exp(log_beta + rev + jnp.concatenate([c] * (t // LANES), axis=1))
        if strict_mask is not None:
            w = jnp.where(strict_mask, w, 0.0)
        w = w.astype(BF16)
        vz = jnp.zeros_like(v)
        v2 = jnp.concatenate([jnp.where(first, v, vz), jnp.where(first, vz, v)], axis=0)
        acc_ref[p] += _dot(jnp.concatenate([w[:t], w[t:]], axis=1), v2)
        c_new = c + jnp.sum(log_keep, axis=-1, keepdims=True)
        c_ref[p] = c_new
        return jnp.max(c_new)

    def step(kb, strict_mask):
        kb_next = jnp.maximum(kb - 1, 0)
        worst = None
        for p in range(SB_STREAMS):
            z = z_ref[p]
            z_ref[p] = logits(p, kb_next)
            done = consume(p, z, kb, strict_mask)
            worst = done if worst is None else jnp.maximum(worst, done)
        return worst

    acc_ref[...] = jnp.zeros_like(acc_ref)
    c_ref[...] = jnp.zeros_like(c_ref)
    for p in range(SB_STREAMS):
        z_ref[p] = logits(p, qi)
    row = lax.broadcasted_iota(jnp.int32, (2 * t, t), 0)
    col = lax.broadcasted_iota(jnp.int32, (2 * t, t), 1)
    cmax = step(qi, col < jnp.where(row >= t, row - t, row))

    def cond(carry):
        kb, cm = carry
        return jnp.logical_and(kb >= 0, cm > SB_LOG_ZERO)

    def body(carry):
        kb, _ = carry
        return kb - 1, step(kb, None)

    lax.while_loop(cond, body, (qi - 1, cmax))
    for p in range(SB_STREAMS):
        o_ref[0, :, p * LANES:(p + 1) * LANES] = acc_ref[p]


def _sb_attention(proj3, u2, t):
    b, s, _ = proj3.shape
    groups = N_SB_PAIRS // SB_STREAMS
    width = SB_STREAMS * LANES
    return pl.pallas_call(
        functools.partial(_sb_kernel, t=t),
        grid=(b, groups, s // t),
        in_specs=[pl.BlockSpec((1, t, width), lambda bi, g, qi: (bi, qi, g)),
                  pl.BlockSpec((1, s, width), lambda bi, g, qi: (bi, 0, groups + g)),
                  pl.BlockSpec((1, s, width), lambda bi, g, qi: (bi, 0, 2 * groups + g)),
                  pl.BlockSpec((2 * t, t), lambda bi, g, qi: (0, 0))],
        out_specs=pl.BlockSpec((1, t, width), lambda bi, g, qi: (bi, qi, g)),
        out_shape=jax.ShapeDtypeStruct((b, s, SB_WIDTH), F32),
        scratch_shapes=[pltpu.VMEM((SB_STREAMS, t, LANES), F32), pltpu.VMEM((SB_STREAMS, 2 * t, LANES), F32),
                        pltpu.VMEM((SB_STREAMS, 2 * t, t), F32)],
        compiler_params=_params("parallel", "parallel", "arbitrary"),
        name="sb_attention",
    )(proj3, proj3, proj3, u2)


def _df_kernel(q_ref, k_ref, v_ref, lam_ref, g_ref, o_ref, acc_ref, m_ref, z_ref, *, t, lambda_init):
    h = pl.program_id(1)
    qi = pl.program_id(2)
    slope = jnp.float32(ALIBI_SLOPES[-1])
    for idx in range(N_DF_HEADS - 2, -1, -1):
        slope = jnp.where(h == idx, jnp.float32(ALIBI_SLOPES[idx]), slope)
    lane = lax.broadcasted_iota(jnp.int32, (t, LANES), 1)
    row = lax.broadcasted_iota(jnp.int32, (t, LANES), 0)
    first = lane < HEAD_DIM
    q = q_ref[0] * SCALE
    zero = jnp.zeros_like(q)
    bias_on = jnp.where(lane < 2, 1.0, 0.0).astype(BF16)
    q2 = jnp.concatenate([jnp.concatenate([jnp.where(first, q, zero), bias_on], axis=1),
                          jnp.concatenate([jnp.where(first, zero, q), bias_on], axis=1)], axis=0)
    key_lo = jnp.where(lane == 0, ((row >> 8) << 8).astype(F32),
                       jnp.where(lane == 1, (row & 255).astype(F32), 0.0)) * slope
    lane0 = lane == 0
    ones_v = jnp.ones((t, LANES), BF16)

    def logits(kb):
        k = k_ref[0, pl.ds(pl.multiple_of(kb * t, t), t), :]
        offset = slope * ((kb - qi) * t).astype(F32)
        k_bias = (key_lo + jnp.where(lane0, offset, 0.0)).astype(BF16)
        return _dot_nt(q2, jnp.concatenate([k, k_bias], axis=1))

    def accumulate(z, kb):
        v = v_ref[0, pl.ds(pl.multiple_of(kb * t, t), t), :]
        m_old = m_ref[...]
        m_new = jnp.maximum(m_old, jnp.max(z, axis=-1, keepdims=True))
        alpha = jnp.exp(m_old - m_new)
        p = jnp.exp(z - jnp.concatenate([m_new] * (t // LANES), axis=1)).astype(BF16)
        pv = _dot(p, jnp.concatenate([v, ones_v], axis=1))
        acc_ref[...] = jnp.concatenate([alpha, alpha], axis=1) * acc_ref[...] + pv
        m_ref[...] = m_new

    acc_ref[...] = jnp.zeros_like(acc_ref)
    m_ref[...] = jnp.full_like(m_ref, -jnp.inf)
    r2 = lax.broadcasted_iota(jnp.int32, (2 * t, t), 0)
    c2 = lax.broadcasted_iota(jnp.int32, (2 * t, t), 1)
    z_ref[...] = jnp.where(c2 <= jnp.where(r2 >= t, r2 - t, r2), logits(qi), -jnp.inf)

    def body(j, carry):
        z = z_ref[...]
        z_ref[...] = logits(j)
        accumulate(z, jnp.where(j == 0, qi, j - 1))
        return carry

    lax.fori_loop(0, qi, body, 0)
    accumulate(z_ref[...], jnp.where(qi == 0, qi, qi - 1))

    lam_vec = lam_ref[...]
    lam = (jnp.exp(jnp.sum(lam_vec[0:1] * lam_vec[1:2], axis=-1, keepdims=True))
           - jnp.exp(jnp.sum(lam_vec[2:3] * lam_vec[3:4], axis=-1, keepdims=True))
           + lambda_init)
    acc = acc_ref[...]
    ratio = acc[:, :LANES] / acc[:, LANES:]
    o = ratio[:t] - lam * ratio[t:]
    o_ref[0] = (_rms(o, g_ref[...]) * (1.0 - lambda_init)).astype(o_ref.dtype)


def _df_attention(proj3, lam_rows, g_df, t, lambda_init):
    b, s, _ = proj3.shape
    col0 = 3 * N_SB_PAIRS
    return pl.pallas_call(
        functools.partial(_df_kernel, t=t, lambda_init=lambda_init),
        grid=(b, N_DF_HEADS, s // t),
        in_specs=[pl.BlockSpec((1, t, LANES), lambda bi, h, qi: (bi, qi, col0 + h)),
                  pl.BlockSpec((1, s, LANES), lambda bi, h, qi: (bi, 0, col0 + N_DF_HEADS + h)),
                  pl.BlockSpec((1, s, LANES), lambda bi, h, qi: (bi, 0, col0 + 2 * N_DF_HEADS + h)),
                  pl.BlockSpec((4, HEAD_DIM), lambda bi, h, qi: (0, 0)),
                  pl.BlockSpec((1, LANES), lambda bi, h, qi: (0, 0))],
        out_specs=pl.BlockSpec((1, t, LANES), lambda bi, h, qi: (bi, qi, h)),
        out_shape=jax.ShapeDtypeStruct((b, s, DF_WIDTH), BF16),
        scratch_shapes=[pltpu.VMEM((2 * t, 2 * LANES), F32), pltpu.VMEM((2 * t, LANES), F32),
                        pltpu.VMEM((2 * t, t), F32)],
        compiler_params=_params("parallel", "parallel", "arbitrary"),
        name="df_attention",
    )(proj3, proj3, proj3, lam_rows, g_df)


def _split_bf16(x):
    hi = x.astype(BF16)
    return hi, (x - hi.astype(F32)).astype(BF16)


def _route(logits):
    lane = lax.broadcasted_iota(jnp.int32, logits.shape, 1)
    lane_f = lane.astype(F32)
    neg = -jnp.inf
    big = float(LANES)
    gl = jnp.where(lane < N_GROUPS, logits, neg)
    gmax = jnp.max(gl, axis=-1, keepdims=True)
    g_idx = jnp.min(jnp.where(gl == gmax, lane_f, big), axis=-1, keepdims=True)
    g_w = 1.0 / jnp.sum(jnp.exp(gl - gmax), axis=-1, keepdims=True)
    e_lane = lane - ROUTER_LANE0
    in_group = jnp.logical_and(jnp.logical_and(e_lane >= 0, e_lane < N_EXPERTS),
                               (e_lane // EXPERTS_PER_GROUP).astype(F32) == g_idx)
    v1 = jnp.where(in_group, logits, neg)
    t1 = jnp.max(v1, axis=-1, keepdims=True)
    i1 = jnp.min(jnp.where(v1 == t1, lane_f, big), axis=-1, keepdims=True)
    v2 = jnp.where(lane_f == i1, neg, v1)
    t2 = jnp.max(v2, axis=-1, keepdims=True)
    i2 = jnp.min(jnp.where(v2 == t2, lane_f, big), axis=-1, keepdims=True)
    e2 = jnp.exp(t2 - t1)
    w1 = g_w / (1.0 + e2)
    w2 = w1 * e2
    return jnp.where(lane == 0, i1 - ROUTER_LANE0, jnp.where(
        lane == 1, i2 - ROUTER_LANE0, jnp.where(lane == 2, w1, jnp.where(lane == 3, w2, 0.0))))


def _run_rows(route):
    lane_f = lax.broadcasted_iota(jnp.int32, route.shape, 1).astype(F32)
    onehot = jnp.where(lane_f == route[:, 0:1], 1.0, jnp.where(lane_f == route[:, 1:2], 1.0, 0.0))
    return jnp.ceil(jnp.sum(onehot, axis=0, keepdims=True) * (1.0 / ROW_GROUP)) * ROW_GROUP, onehot


def _outproj_kernel(osb_ref, odf_ref, x_ref, gsb_ref, wsb_ref, wdf_ref, gffn_ref, wr_ref, br_ref,
                    x1_ref, h2_ref, route_ref, runs_ref):
    a_sb = _rms(osb_ref[...], gsb_ref[...]).astype(BF16)
    x1 = x_ref[...] + _dot(a_sb, wsb_ref[...]) + _dot(odf_ref[...], wdf_ref[...])
    x1_ref[...] = x1
    h2 = _rms(x1, gffn_ref[...])
    h2_ref[...] = h2.astype(BF16)
    hi, lo = _split_bf16(h2)
    logits = _rows_dot(_dot, jnp.concatenate([hi, hi, lo], axis=1), wr_ref[...], hi.shape[0] // 2) + br_ref[...]
    route = _route(logits)
    route_ref[...] = route
    runs_ref[...] = jnp.broadcast_to(_run_rows(route)[0], runs_ref.shape)


def _outproj(o_sb, o_df, x2d, g_sb, w_sb, w_df, g_ffn, w_router3, b_router, tm):
    n, d = x2d.shape
    row = lambda i: (i, 0)
    const = lambda i: (0, 0)
    return pl.pallas_call(
        _outproj_kernel,
        grid=(n // tm,),
        in_specs=[pl.BlockSpec((tm, SB_WIDTH), row), pl.BlockSpec((tm, DF_WIDTH), row),
                  pl.BlockSpec((tm, d), row), pl.BlockSpec((1, SB_WIDTH), const),
                  pl.BlockSpec((SB_WIDTH, d), const), pl.BlockSpec((DF_WIDTH, d), const),
                  pl.BlockSpec((1, d), const), pl.BlockSpec((3 * d, LANES), const),
                  pl.BlockSpec((1, LANES), const)],
        out_specs=[pl.BlockSpec((tm, d), row), pl.BlockSpec((tm, d), row),
                   pl.BlockSpec((tm, LANES), row), pl.BlockSpec((META_ROWS, LANES), row)],
        out_shape=[jax.ShapeDtypeStruct((n, d), F32), jax.ShapeDtypeStruct((n, d), BF16),
                   jax.ShapeDtypeStruct((n, LANES), F32),
                   jax.ShapeDtypeStruct((n // tm * META_ROWS, LANES), F32)],
        compiler_params=_params("parallel"),
        name="outproj_router",
    )(o_sb, o_df, x2d, g_sb, w_sb, w_df, g_ffn, w_router3, b_router)


RUN_SIZES = (512, 256, 128, 64, 32, 16, 8)


def _lane_prefix(x_groups):
    r = lax.broadcasted_iota(jnp.int32, (LANES, LANES), 0)
    c = lax.broadcasted_iota(jnp.int32, (LANES, LANES), 1)
    return _dot(x_groups.astype(BF16), jnp.where(r < c, 1.0, 0.0).astype(BF16))


def _plan_kernel(route_ref, runs_ref, pos_ref, meta_ref, seg_ref, post_ref, run_ref, off_ref, *, tm, tile_rows):
    i = pl.program_id(0)
    route = route_ref[...]
    lane = lax.broadcasted_iota(jnp.int32, (tm, LANES), 1)
    lane_f = lane.astype(F32)
    sel1 = lane_f == route[:, 0:1]
    sel2 = lane_f == route[:, 1:2]
    run_rows, onehot = _run_rows(route)
    mrow = lax.broadcasted_iota(jnp.int32, (META_ROWS, LANES), 0)

    @pl.when(i == 0)
    def _():
        cnt = jnp.sum(runs_ref[...], axis=0, keepdims=True) * (1.0 / META_ROWS)
        cnt = jnp.broadcast_to(cnt, (META_ROWS, LANES))
        n_tiles = jnp.ceil(cnt * (1.0 / tile_rows))
        tile_off = _lane_prefix(n_tiles)
        off_ref[...] = tile_off * tile_rows
        run_ref[...] = jnp.zeros_like(run_ref)
        meta_ref[...] = jnp.where(mrow == 0, tile_off, jnp.where(mrow == 1, n_tiles,
                                                                 jnp.where(mrow == 2, cnt, 0.0)))

    r = lax.broadcasted_iota(jnp.int32, (tm, tm), 0)
    c = lax.broadcasted_iota(jnp.int32, (tm, tm), 1)
    earlier = _dot(jnp.where(c < r, 1.0, 0.0).astype(BF16), onehot.astype(BF16))
    local_start = _lane_prefix(jnp.broadcast_to(run_rows * (1.0 / ROW_GROUP), (META_ROWS, LANES))) * ROW_GROUP
    global_start = run_ref[...] + off_ref[...]
    base_g = earlier + global_start[0:1]
    base_l = earlier + local_start[0:1]

    def pick(sel, base):
        return jnp.sum(jnp.where(sel, base, 0.0), axis=-1, keepdims=True)

    pos = jnp.where(lane == 0, pick(sel1, base_g), jnp.where(
        lane == 1, pick(sel2, base_g), jnp.where(
            lane == 2, pick(sel1, base_l), jnp.where(lane == 3, pick(sel2, base_l), 0.0))))
    pos_ref[...] = pos.astype(jnp.int32)
    post_ref[...] = jnp.transpose(pos)[:META_ROWS].astype(jnp.int32)
    seg_ref[...] = jnp.where(mrow == 0, local_start, jnp.where(
        mrow == 1, run_rows, jnp.where(mrow == 2, global_start, 0.0))).astype(jnp.int32)
    run_ref[...] += run_rows


def _plan(route, runs, tm, tile_rows):
    n = route.shape[0]
    steps = n // tm
    assert (n + (ROW_GROUP - 1) * steps) // tile_rows + 1 <= 256, "per-expert tile counts must stay exact in bf16"
    assert (2 * tm) // ROW_GROUP + N_EXPERTS <= 256, "per-tile run sizes must stay exact in bf16"
    small = pltpu.VMEM((META_ROWS, LANES), F32)
    tile = lambda i: (i, 0)
    return pl.pallas_call(
        functools.partial(_plan_kernel, tm=tm, tile_rows=tile_rows),
        grid=(steps,),
        in_specs=[pl.BlockSpec((tm, LANES), tile),
                  pl.BlockSpec((steps * META_ROWS, LANES), lambda i: (0, 0))],
        out_specs=[pl.BlockSpec((tm, LANES), tile), pl.BlockSpec((META_ROWS, LANES), lambda i: (0, 0)),
                   pl.BlockSpec((META_ROWS, LANES), tile), pl.BlockSpec((META_ROWS, tm), tile)],
        out_shape=[jax.ShapeDtypeStruct((n, LANES), jnp.int32),
                   jax.ShapeDtypeStruct((META_ROWS, LANES), F32),
                   jax.ShapeDtypeStruct((steps * META_ROWS, LANES), jnp.int32),
                   jax.ShapeDtypeStruct((steps * META_ROWS, tm), jnp.int32)],
        scratch_shapes=[small, small],
        compiler_params=_params("arbitrary"),
        name="moe_plan",
    )(route, runs)


def _row_copy(src_ref, src_row, dst_ref, dst_row, sem):
    return pltpu.make_async_copy(src_ref.at[pl.ds(src_row, 1)], dst_ref.at[pl.ds(dst_row, 1)], sem)


def _for_each_run_piece(seg_ref, fn):
    def per_expert(e, carry):
        local = seg_ref[0, 0, e]
        length = seg_ref[0, 1, e]
        dst = seg_ref[0, 2, e]
        done = jnp.int32(0)
        for size in RUN_SIZES:
            has = (length & size) != 0

            @pl.when(has)
            def _(size=size, done=done):
                fn(pl.multiple_of(local + done, ROW_GROUP), pl.multiple_of(dst + done, ROW_GROUP), size)

            done = done + jnp.where(has, size, 0)
        return carry

    lax.fori_loop(0, N_EXPERTS, per_expert, 0)


def _dispatch_kernel(meta_ref, h_ref, pos_ref, seg_ref, seg_prev_ref, xs_ref, local_ref, zero_ref, sem, fill_sem,
                     *, tm, tile_rows):
    i = pl.program_id(0)
    last = pl.num_programs(0) - 1
    buf = lax.rem(i, 2)
    rows = local_ref.shape[1]
    local_pos = jnp.transpose(pos_ref[...].astype(F32))
    lp1 = local_pos[2:3].astype(jnp.int32)
    lp2 = local_pos[3:4].astype(jnp.int32)
    r = lax.broadcasted_iota(jnp.int32, (rows, tm), 0)
    place = jnp.where(r == lp1, 1.0, jnp.where(r == lp2, 1.0, 0.0)).astype(BF16)
    local_ref[buf] = _dot(place, h_ref[...])

    def run_copy(b):
        def make(local, dst, size):
            return pltpu.make_async_copy(local_ref.at[b, pl.ds(local, size)], xs_ref.at[pl.ds(dst, size)],
                                         sem.at[b])
        return make

    _for_each_run_piece(seg_ref, lambda local, dst, size: run_copy(buf)(local, dst, size).start())

    @pl.when(i > 0)
    def _():
        _for_each_run_piece(seg_prev_ref, lambda local, dst, size: run_copy(1 - buf)(local, dst, size).wait())

    @pl.when(i == last)
    def _():
        _for_each_run_piece(seg_ref, lambda local, dst, size: run_copy(buf)(local, dst, size).wait())
        zero_ref[...] = jnp.zeros_like(zero_ref)

        def per_expert(e, carry):
            off = meta_ref[0, e] * tile_rows
            lo = meta_ref[2, e]
            hi = meta_ref[1, e] * tile_rows

            def fill(g, c):
                pltpu.make_async_copy(zero_ref.at[pl.ds(0, ROW_GROUP)],
                                      xs_ref.at[pl.ds(pl.multiple_of(off + g * ROW_GROUP, ROW_GROUP), ROW_GROUP)],
                                      fill_sem).start()
                return c

            def drain(g, c):
                pltpu.make_async_copy(zero_ref.at[pl.ds(0, ROW_GROUP)], xs_ref.at[pl.ds(0, ROW_GROUP)],
                                      fill_sem).wait()
                return c

            lax.fori_loop(lo // ROW_GROUP, hi // ROW_GROUP, fill, 0)
            lax.fori_loop(lo // ROW_GROUP, hi // ROW_GROUP, drain, 0)
            return carry

        lax.fori_loop(0, N_EXPERTS, per_expert, 0)

        def spare_tile(j, carry):
            c = pltpu.make_async_copy(zero_ref, xs_ref.at[pl.ds(j * tile_rows, tile_rows)], fill_sem)
            c.start()
            c.wait()
            return carry

        n_used = meta_ref[0, N_EXPERTS - 1] + meta_ref[1, N_EXPERTS - 1]
        lax.fori_loop(n_used, xs_ref.shape[0] // tile_rows, spare_tile, 0)


def _dispatch(meta_i, h2, pos, seg3, n_rows, tm, tile_rows):
    n, d = h2.shape
    local_rows = 2 * tm + N_EXPERTS * ROW_GROUP
    seg_spec = lambda index_map: pl.BlockSpec((1, META_ROWS, LANES), index_map, memory_space=pltpu.SMEM)
    return pl.pallas_call(
        functools.partial(_dispatch_kernel, tm=tm, tile_rows=tile_rows),
        grid_spec=pltpu.PrefetchScalarGridSpec(
            num_scalar_prefetch=1,
            grid=(n // tm,),
            in_specs=[pl.BlockSpec((tm, d), lambda i, meta: (i, 0)),
                      pl.BlockSpec((tm, LANES), lambda i, meta: (i, 0)),
                      seg_spec(lambda i, meta: (i, 0, 0)),
                      seg_spec(lambda i, meta: (jnp.maximum(i - 1, 0), 0, 0))],
            out_specs=pl.BlockSpec(memory_space=pl.ANY),
            scratch_shapes=[pltpu.VMEM((2, local_rows, d), F32), pltpu.VMEM((tile_rows, d), F32),
                            pltpu.SemaphoreType.DMA((2,)), pltpu.SemaphoreType.DMA(())]),
        out_shape=jax.ShapeDtypeStruct((n_rows, d), F32),
        compiler_params=_params("arbitrary"),
        name="moe_dispatch",
    )(meta_i, h2, pos, seg3, seg3)


def _gmm_kernel(te_ref, nv_ref, x_ref, wg_ref, wu_ref, wd_ref, y_ref, wg_bf, wu_bf, wd_bf):
    j = pl.program_id(0)
    used = j < nv_ref[0]
    new_expert = jnp.logical_or(j == 0, te_ref[j] != te_ref[jnp.maximum(j - 1, 0)])

    @pl.when(jnp.logical_and(used, new_expert))
    def _():
        wg_bf[...] = wg_ref[0].astype(BF16)
        wu_bf[...] = wu_ref[0].astype(BF16)
        wd_bf[...] = wd_ref[0].astype(BF16)

    @pl.when(used)
    def _():
        x = x_ref[...].astype(BF16)
        a = _dot(x, wg_bf[...])
        hid = (a * jax.nn.sigmoid(a)) * _dot(x, wu_bf[...])
        y_ref[...] = _dot(hid.astype(BF16), wd_bf[...])

    @pl.when(jnp.logical_not(used))
    def _():
        y_ref[...] = jnp.zeros_like(y_ref)


def _gmm(tile_expert, n_valid, xs, wg, wu, wd, tile_rows):
    n_rows, d = xs.shape
    de = wg.shape[2]
    rows = lambda j, te, nv: (jnp.minimum(j, nv[0] - 1), 0)
    return pl.pallas_call(
        _gmm_kernel,
        grid_spec=pltpu.PrefetchScalarGridSpec(
            num_scalar_prefetch=2,
            grid=(n_rows // tile_rows,),
            in_specs=[pl.BlockSpec((tile_rows, d), rows),
                      pl.BlockSpec((1, d, de), lambda j, te, nv: (te[j], 0, 0)),
                      pl.BlockSpec((1, d, de), lambda j, te, nv: (te[j], 0, 0)),
                      pl.BlockSpec((1, de, d), lambda j, te, nv: (te[j], 0, 0))],
            out_specs=pl.BlockSpec((tile_rows, d), lambda j, te, nv: (j, 0)),
            scratch_shapes=[pltpu.VMEM((d, de), BF16), pltpu.VMEM((d, de), BF16), pltpu.VMEM((de, d), BF16)]),
        out_shape=jax.ShapeDtypeStruct((n_rows, d), F32),
        compiler_params=_params("arbitrary"),
        name="moe_experts",
    )(tile_expert, n_valid, xs, wg, wu, wd)


def _ple_kernel(x1_ref, p_ref, route_ref, pos_ref, pos_next_ref, ys_ref, gple_ref, wgate_ref, wproj_ref, gfin_ref,
                o_ref, y_ref, sem, *, tm):
    i = pl.program_id(0)
    buf = lax.rem(i, 2)

    def gather(pos, b):
        def start(g, carry):
            base = pl.multiple_of(g * ROW_GROUP, ROW_GROUP)
            for j in range(ROW_GROUP):
                for slot in range(2):
                    src = pos[0, slot, base + j]
                    _row_copy(ys_ref, src, y_ref.at[b, slot], base + j, sem.at[b]).start(priority=slot)
            return carry

        lax.fori_loop(0, tm // ROW_GROUP, start, 0)

    @pl.when(i == 0)
    def _():
        gather(pos_ref, 0)

    @pl.when(i + 1 < pl.num_programs(0))
    def _():
        gather(pos_next_ref, 1 - buf)

    def wait(r, carry):
        _row_copy(ys_ref, 0, y_ref.at[buf, 0], 0, sem.at[buf]).wait()
        return carry

    emb = _dot(p_ref[...].astype(BF16), wproj_ref[...])
    lax.fori_loop(0, 2 * tm, wait, 0, unroll=8)
    route = route_ref[...]
    x = x1_ref[...] + (route[:, 2:3] * y_ref[buf, 0] + route[:, 3:4] * y_ref[buf, 1])
    gate = jax.nn.sigmoid(_dot(_rms(x, gple_ref[...]).astype(BF16), wgate_ref[...]))
    o_ref[...] = _rms(x + gate * emb, gfin_ref[...])


def _ple(x1, p2d, route, pos3, ys, g_ple, w_gate, w_proj, g_final, tm):
    n, d = x1.shape
    pd = p2d.shape[1]
    steps = n // tm
    row = lambda i: (i, 0)
    const = lambda i: (0, 0)
    pos_spec = lambda index_map: pl.BlockSpec((1, META_ROWS, tm), index_map, memory_space=pltpu.SMEM)
    return pl.pallas_call(
        functools.partial(_ple_kernel, tm=tm),
        grid=(steps,),
        in_specs=[pl.BlockSpec((tm, d), row), pl.BlockSpec((tm, pd), row), pl.BlockSpec((tm, LANES), row),
                  pos_spec(lambda i: (i, 0, 0)), pos_spec(lambda i: (jnp.minimum(i + 1, steps - 1), 0, 0)),
                  pl.BlockSpec(memory_space=pl.ANY), pl.BlockSpec((1, d), const),
                  pl.BlockSpec((d, d), const), pl.BlockSpec((pd, d), const), pl.BlockSpec((1, d), const)],
        out_specs=pl.BlockSpec((tm, d), row),
        out_shape=jax.ShapeDtypeStruct((n, d), F32),
        scratch_shapes=[pltpu.VMEM((2, 2, tm, d), F32), pltpu.SemaphoreType.DMA((2,))],
        compiler_params=_params("arbitrary"),
        name="combine_ple_final",
    )(x1, p2d, route, pos3, pos3, ys, g_ple, w_gate, w_proj, g_final)


def _layer(x, p_i, layer_idx, g_mix, w_in, lambda_q1, lambda_k1, lambda_q2, lambda_k2, g_sb_out, g_df_out,
           w_out, g_ffn, w_router_group, b_router_group, w_router_expert, b_router_expert, w_expert_gate,
           w_expert_up, w_expert_down, g_ple, w_ple_gate, w_ple_proj, g_out):
    b, s, d = x.shape
    n = b * s
    tm = min(512, n)
    t = min(256, s)
    x2d = x.reshape(n, d)
    lambda_init = 0.8 - 0.6 * math.exp(-0.3 * layer_idx)

    proj = _inproj(x2d, g_mix.reshape(1, d), w_in.astype(BF16), tm)
    proj3 = proj.reshape(b, s, proj.shape[1])

    tri = (lax.broadcasted_iota(jnp.int32, (t, t), 0) > lax.broadcasted_iota(jnp.int32, (t, t), 1))
    u2 = jnp.concatenate([tri, tri], axis=0).astype(BF16)
    o_sb = _sb_attention(proj3, u2, t).reshape(n, SB_WIDTH)

    lam_rows = jnp.stack([lambda_q1, lambda_k1, lambda_q2, lambda_k2]).astype(F32)
    o_df = _df_attention(proj3, lam_rows, g_df_out.reshape(1, LANES).astype(F32), min(512, s),
                         lambda_init).reshape(n, DF_WIDTH)

    w_router = jnp.concatenate([w_router_group, w_router_expert], axis=1).astype(F32)
    w_router = jnp.pad(w_router, ((0, 0), (0, LANES - w_router.shape[1])))
    wr_hi, wr_lo = _split_bf16(w_router)
    b_router = jnp.pad(jnp.concatenate([b_router_group, b_router_expert]).astype(F32),
                       (0, LANES - N_GROUPS - N_EXPERTS)).reshape(1, LANES)
    w_out_bf = w_out.astype(BF16)
    x1, h2, route, runs = _outproj(o_sb, o_df, x2d, g_sb_out.reshape(1, SB_WIDTH), w_out_bf[:SB_WIDTH],
                             w_out_bf[SB_WIDTH:], g_ffn.reshape(1, d),
                             jnp.concatenate([wr_hi, wr_lo, wr_hi], axis=0), b_router, tm)

    tile_rows = min(MOE_TILE_ROWS, n)
    pos, meta, seg, pos_t = _plan(route, runs, tm, tile_rows)
    run_padding = (n // tm) * N_EXPERTS * ROW_GROUP
    n_tiles = pl.cdiv(2 * n + run_padding, tile_rows) + N_EXPERTS
    meta_i = meta[:3, :N_EXPERTS].astype(jnp.int32)
    ends = meta_i[0] + meta_i[1]
    n_valid = ends[-1:]
    tile_ids = jnp.arange(n_tiles, dtype=jnp.int32)
    tile_expert = jnp.sum(tile_ids[:, None] >= ends[None, :], axis=1).astype(jnp.int32)
    tile_expert = jnp.where(tile_ids < n_valid, tile_expert, tile_expert[n_valid[0] - 1])
    xs = _dispatch(meta_i, h2, pos, seg.reshape(n // tm, META_ROWS, LANES), n_tiles * tile_rows, tm, tile_rows)
    ys = _gmm(tile_expert, n_valid, xs, w_expert_gate, w_expert_up, w_expert_down, tile_rows)

    out = _ple(x1, p_i.reshape(n, p_i.shape[-1]), route, pos_t.reshape(n // tm, META_ROWS, tm), ys,
               g_ple.reshape(1, d), w_ple_gate.astype(BF16), w_ple_proj.astype(BF16), g_out.reshape(1, d), tm)
    return out.reshape(b, s, d)


def kernel(x, p, g_mix, w_in, lambda_q1, lambda_k1, lambda_q2, lambda_k2, g_sb_out, g_df_out, w_out, g_ffn,
           w_router_group, b_router_group, w_router_expert, b_router_expert, w_expert_gate, w_expert_up,
           w_expert_down, g_ple, w_ple_gate, w_ple_proj, g_final):
    depth = p.shape[0]
    assert depth == 1, "the final norm is fused into the single layer's last kernel"
    return _layer(x, p[0], 0, g_mix[0], w_in[0], lambda_q1[0], lambda_k1[0], lambda_q2[0], lambda_k2[0],
                  g_sb_out[0], g_df_out[0], w_out[0], g_ffn[0], w_router_group[0], b_router_group[0],
                  w_router_expert[0], b_router_expert[0], w_expert_gate[0], w_expert_up[0],
                  w_expert_down[0], g_ple[0], w_ple_gate[0], w_ple_proj[0], g_final)
```

```python
import functools
import math

import jax
import jax.numpy as jnp
from jax import lax
from jax.experimental import pallas as pl
from jax.experimental.pallas import tpu as pltpu

F32 = jnp.float32
BF16 = jnp.bfloat16

HEAD_DIM = 64
LANES = 128
N_SB_PAIRS = 4
N_DF_HEADS = 4
SB_WIDTH = 512
DF_WIDTH = 512
SCALE = HEAD_DIM ** -0.5
NORM_EPS = 1e-6
N_GROUPS = 4
EXPERTS_PER_GROUP = 4
N_EXPERTS = 16
ROUTER_LANE0 = N_GROUPS
ALIBI_SLOPES = tuple(2.0 ** (-8.0 * (h + 1) / N_DF_HEADS) for h in range(N_DF_HEADS))
SB_LOG_ZERO = -110.0
VMEM_LIMIT = 48 * 1024 * 1024
MOE_TILE_ROWS = 512
META_ROWS = 8
ROW_GROUP = 8


def _rms(x, g):
    ms = jnp.mean(x * x, axis=-1, keepdims=True)
    return x * lax.rsqrt(ms + NORM_EPS) * g


def _dot(a, b):
    return jnp.dot(a, b, preferred_element_type=F32)


def _dot_nt(a, b):
    return lax.dot_general(a, b, (((1,), (1,)), ((), ())), preferred_element_type=F32)


def _rows_dot(dot, a, b, rows):
    return jnp.concatenate([dot(a[r:r + rows], b) for r in range(0, a.shape[0], rows)], axis=0)


def _params(*sem):
    return pltpu.CompilerParams(dimension_semantics=sem, vmem_limit_bytes=VMEM_LIMIT)


def _inproj_kernel(x_ref, g_ref, w_ref, o_ref, *, tn):
    h = _rms(x_ref[...], g_ref[...]).astype(BF16)
    for j in range(o_ref.shape[1] // tn):
        o_ref[:, j * tn:(j + 1) * tn] = _dot(h, w_ref[:, j * tn:(j + 1) * tn]).astype(o_ref.dtype)


def _inproj(x2d, g, w_bf16, tm):
    n, d = x2d.shape
    width = w_bf16.shape[1]
    return pl.pallas_call(
        functools.partial(_inproj_kernel, tn=1024),
        grid=(n // tm,),
        in_specs=[pl.BlockSpec((tm, d), lambda i: (i, 0)),
                  pl.BlockSpec((1, d), lambda i: (0, 0)),
                  pl.BlockSpec((d, width), lambda i: (0, 0))],
        out_specs=pl.BlockSpec((tm, width), lambda i: (i, 0)),
        out_shape=jax.ShapeDtypeStruct((n, width), BF16),
        compiler_params=_params("parallel"),
        name="inproj",
    )(x2d, g, w_bf16)


SB_STREAMS = 4


def _sb_kernel(q_ref, k_ref, v_ref, u_ref, o_ref, acc_ref, c_ref, z_ref, *, t):
    qi = pl.program_id(2)
    lane = lax.broadcasted_iota(jnp.int32, (t, LANES), 1)
    first = lane < HEAD_DIM

    def stacked_q(p):
        q = q_ref[0, :, p * LANES:(p + 1) * LANES] * SCALE
        zero = jnp.zeros_like(q)
        return jnp.concatenate([jnp.where(first, q, zero), jnp.where(first, zero, q)], axis=0)

    q2 = [stacked_q(p) for p in range(SB_STREAMS)]

    def logits(p, kb):
        k = k_ref[0, pl.ds(pl.multiple_of(kb * t, t), t), p * LANES:(p + 1) * LANES]
        return _rows_dot(_dot_nt, q2[p], k, t)

    def consume(p, z, kb, strict_mask):
        v = v_ref[0, pl.ds(pl.multiple_of(kb * t, t), t), p * LANES:(p + 1) * LANES]
        soft = jnp.log(1.0 + jnp.exp(-jnp.abs(z)))
        log_beta = jnp.minimum(z, 0.0) - soft
        log_keep = log_beta - z
        if strict_mask is not None:
            log_keep = jnp.where(strict_mask, log_keep, 0.0)
        hi = log_keep.astype(BF16)
        lo = (log_keep - hi.astype(F32)).astype(BF16)
        rev = _rows_dot(_dot, jnp.concatenate([hi, lo], axis=1), u_ref[...], t)
        c = c_ref[p]
        w = jnp.exp(log_beta + rev + jnp.concatenate([c] * (t // LANES), axis=1))
        if strict_mask is not None:
            w = jnp.where(strict_mask, w, 0.0)
        w = w.astype(BF16)
        vz = jnp.zeros_like(v)
        v2 = jnp.concatenate([jnp.where(first, v, vz), jnp.where(first, vz, v)], axis=0)
        acc_ref[p] += _dot(jnp.concatenate([w[:t], w[t:]], axis=1), v2)
        c_new = c + jnp.sum(log_keep, axis=-1, keepdims=True)
        c_ref[p] = c_new
        return jnp.max(c_new)

    def step(kb, strict_mask):
        kb_next = jnp.maximum(kb - 1, 0)
        worst = None
        for p in range(SB_STREAMS):
            z = z_ref[p]
            z_ref[p] = logits(p, kb_next)
            done = consume(p, z, kb, strict_mask)
            worst = done if worst is None else jnp.maximum(worst, done)
        return worst

    acc_ref[...] = jnp.zeros_like(acc_ref)
    c_ref[...] = jnp.zeros_like(c_ref)
    for p in range(SB_STREAMS):
        z_ref[p] = logits(p, qi)
    row = lax.broadcasted_iota(jnp.int32, (2 * t, t), 0)
    col = lax.broadcasted_iota(jnp.int32, (2 * t, t), 1)
    cmax = step(qi, col < jnp.where(row >= t, row - t, row))

    def cond(carry):
        kb, cm = carry
        return jnp.logical_and(kb >= 0, cm > SB_LOG_ZERO)

    def body(carry):
        kb, _ = carry
        return kb - 1, step(kb, None)

    lax.while_loop(cond, body, (qi - 1, cmax))
    for p in range(SB_STREAMS):
        o_ref[0, :, p * LANES:(p + 1) * LANES] = acc_ref[p]


def _sb_attention(proj3, u2, t):
    b, s, _ = proj3.shape
    groups = N_SB_PAIRS // SB_STREAMS
    width = SB_STREAMS * LANES
    return pl.pallas_call(
        functools.partial(_sb_kernel, t=t),
        grid=(b, groups, s // t),
        in_specs=[pl.BlockSpec((1, t, width), lambda bi, g, qi: (bi, qi, g)),
                  pl.BlockSpec((1, s, width), lambda bi, g, qi: (bi, 0, groups + g)),
                  pl.BlockSpec((1, s, width), lambda bi, g, qi: (bi, 0, 2 * groups + g)),
                  pl.BlockSpec((2 * t, t), lambda bi, g, qi: (0, 0))],
        out_specs=pl.BlockSpec((1, t, width), lambda bi, g, qi: (bi, qi, g)),
        out_shape=jax.ShapeDtypeStruct((b, s, SB_WIDTH), F32),
        scratch_shapes=[pltpu.VMEM((SB_STREAMS, t, LANES), F32), pltpu.VMEM((SB_STREAMS, 2 * t, LANES), F32),
                        pltpu.VMEM((SB_STREAMS, 2 * t, t), F32)],
        compiler_params=_params("parallel", "parallel", "arbitrary"),
        name="sb_attention",
    )(proj3, proj3, proj3, u2)


def _df_kernel(q_ref, k_ref, v_ref, lam_ref, g_ref, o_ref, acc_ref, m_ref, z_ref, *, t, lambda_init):
    h = pl.program_id(1)
    qi = pl.program_id(2)
    slope = jnp.float32(ALIBI_SLOPES[-1])
    for idx in range(N_DF_HEADS - 2, -1, -1):
        slope = jnp.where(h == idx, jnp.float32(ALIBI_SLOPES[idx]), slope)
    lane = lax.broadcasted_iota(jnp.int32, (t, LANES), 1)
    row = lax.broadcasted_iota(jnp.int32, (t, LANES), 0)
    first = lane < HEAD_DIM
    q = q_ref[0] * SCALE
    zero = jnp.zeros_like(q)
    bias_on = jnp.where(lane < 2, 1.0, 0.0).astype(BF16)
    q2 = jnp.concatenate([jnp.concatenate([jnp.where(first, q, zero), bias_on], axis=1),
                          jnp.concatenate([jnp.where(first, zero, q), bias_on], axis=1)], axis=0)
    key_lo = jnp.where(lane == 0, ((row >> 8) << 8).astype(F32),
                       jnp.where(lane == 1, (row & 255).astype(F32), 0.0)) * slope
    lane0 = lane == 0
    ones_v = jnp.ones((t, LANES), BF16)

    def logits(kb):
        k = k_ref[0, pl.ds(pl.multiple_of(kb * t, t), t), :]
        offset = slope * ((kb - qi) * t).astype(F32)
        k_bias = (key_lo + jnp.where(lane0, offset, 0.0)).astype(BF16)
        return _dot_nt(q2, jnp.concatenate([k, k_bias], axis=1))

    def accumulate(z, kb):
        v = v_ref[0, pl.ds(pl.multiple_of(kb * t, t), t), :]
        m_old = m_ref[...]
        m_new = jnp.maximum(m_old, jnp.max(z, axis=-1, keepdims=True))
        alpha = jnp.exp(m_old - m_new)
        p = jnp.exp(z - jnp.concatenate([m_new] * (t // LANES), axis=1)).astype(BF16)
        pv = _dot(p, jnp.concatenate([v, ones_v], axis=1))
        acc_ref[...] = jnp.concatenate([alpha, alpha], axis=1) * acc_ref[...] + pv
        m_ref[...] = m_new

    acc_ref[...] = jnp.zeros_like(acc_ref)
    m_ref[...] = jnp.full_like(m_ref, -jnp.inf)
    r2 = lax.broadcasted_iota(jnp.int32, (2 * t, t), 0)
    c2 = lax.broadcasted_iota(jnp.int32, (2 * t, t), 1)
    z_ref[...] = jnp.where(c2 <= jnp.where(r2 >= t, r2 - t, r2), logits(qi), -jnp.inf)

    def body(j, carry):
        z = z_ref[...]
        z_ref[...] = logits(j)
        accumulate(z, jnp.where(j == 0, qi, j - 1))
        return carry

    lax.fori_loop(0, qi, body, 0)
    accumulate(z_ref[...], jnp.where(qi == 0, qi, qi - 1))

    lam_vec = lam_ref[...]
    lam = (jnp.exp(jnp.sum(lam_vec[0:1] * lam_vec[1:2], axis=-1, keepdims=True))
           - jnp.exp(jnp.sum(lam_vec[2:3] * lam_vec[3:4], axis=-1, keepdims=True))
           + lambda_init)
    acc = acc_ref[...]
    ratio = acc[:, :LANES] / acc[:, LANES:]
    o = ratio[:t] - lam * ratio[t:]
    o_ref[0] = (_rms(o, g_ref[...]) * (1.0 - lambda_init)).astype(o_ref.dtype)


def _df_attention(proj3, lam_rows, g_df, t, lambda_init):
    b, s, _ = proj3.shape
    col0 = 3 * N_SB_PAIRS
    return pl.pallas_call(
        functools.partial(_df_kernel, t=t, lambda_init=lambda_init),
        grid=(b, N_DF_HEADS, s // t),
        in_specs=[pl.BlockSpec((1, t, LANES), lambda bi, h, qi: (bi, qi, col0 + h)),
                  pl.BlockSpec((1, s, LANES), lambda bi, h, qi: (bi, 0, col0 + N_DF_HEADS + h)),
                  pl.BlockSpec((1, s, LANES), lambda bi, h, qi: (bi, 0, col0 + 2 * N_DF_HEADS + h)),
                  pl.BlockSpec((4, HEAD_DIM), lambda bi, h, qi: (0, 0)),
                  pl.BlockSpec((1, LANES), lambda bi, h, qi: (0, 0))],
        out_specs=pl.BlockSpec((1, t, LANES), lambda bi, h, qi: (bi, qi, h)),
        out_shape=jax.ShapeDtypeStruct((b, s, DF_WIDTH), BF16),
        scratch_shapes=[pltpu.VMEM((2 * t, 2 * LANES), F32), pltpu.VMEM((2 * t, LANES), F32),
                        pltpu.VMEM((2 * t, t), F32)],
        compiler_params=_params("parallel", "parallel", "arbitrary"),
        name="df_attention",
    )(proj3, proj3, proj3, lam_rows, g_df)


def _split_bf16(x):
    hi = x.astype(BF16)
    return hi, (x - hi.astype(F32)).astype(BF16)


def _route(logits):
    lane = lax.broadcasted_iota(jnp.int32, logits.shape, 1)
    lane_f = lane.astype(F32)
    neg = -jnp.inf
    big = float(LANES)
    gl = jnp.where(lane < N_GROUPS, logits, neg)
    gmax = jnp.max(gl, axis=-1, keepdims=True)
    g_idx = jnp.min(jnp.where(gl == gmax, lane_f, big), axis=-1, keepdims=True)
    g_w = 1.0 / jnp.sum(jnp.exp(gl - gmax), axis=-1, keepdims=True)
    e_lane = lane - ROUTER_LANE0
    in_group = jnp.logical_and(jnp.logical_and(e_lane >= 0, e_lane < N_EXPERTS),
                               (e_lane // EXPERTS_PER_GROUP).astype(F32) == g_idx)
    v1 = jnp.where(in_group, logits, neg)
    t1 = jnp.max(v1, axis=-1, keepdims=True)
    i1 = jnp.min(jnp.where(v1 == t1, lane_f, big), axis=-1, keepdims=True)
    v2 = jnp.where(lane_f == i1, neg, v1)
    t2 = jnp.max(v2, axis=-1, keepdims=True)
    i2 = jnp.min(jnp.where(v2 == t2, lane_f, big), axis=-1, keepdims=True)
    e2 = jnp.exp(t2 - t1)
    w1 = g_w / (1.0 + e2)
    w2 = w1 * e2
    return jnp.where(lane == 0, i1 - ROUTER_LANE0, jnp.where(
        lane == 1, i2 - ROUTER_LANE0, jnp.where(lane == 2, w1, jnp.where(lane == 3, w2, 0.0))))


def _run_rows(route):
    lane_f = lax.broadcasted_iota(jnp.int32, route.shape, 1).astype(F32)
    onehot = jnp.where(lane_f == route[:, 0:1], 1.0, jnp.where(lane_f == route[:, 1:2], 1.0, 0.0))
    return jnp.ceil(jnp.sum(onehot, axis=0, keepdims=True) * (1.0 / ROW_GROUP)) * ROW_GROUP, onehot


def _outproj_kernel(osb_ref, odf_ref, x_ref, gsb_ref, wsb_ref, wdf_ref, gffn_ref, wr_ref, br_ref,
                    x1_ref, h2_ref, route_ref, runs_ref):
    a_sb = _rms(osb_ref[...], gsb_ref[...]).astype(BF16)
    x1 = x_ref[...] + _dot(a_sb, wsb_ref[...]) + _dot(odf_ref[...], wdf_ref[...])
    x1_ref[...] = x1
    h2 = _rms(x1, gffn_ref[...])
    h2_ref[...] = h2.astype(BF16)
    hi, lo = _split_bf16(h2)
    logits = _rows_dot(_dot, jnp.concatenate([hi, hi, lo], axis=1), wr_ref[...], hi.shape[0] // 2) + br_ref[...]
    route = _route(logits)
    route_ref[...] = route
    runs_ref[...] = jnp.broadcast_to(_run_rows(route)[0], runs_ref.shape)


def _outproj(o_sb, o_df, x2d, g_sb, w_sb, w_df, g_ffn, w_router3, b_router, tm):
    n, d = x2d.shape
    row = lambda i: (i, 0)
    const = lambda i: (0, 0)
    return pl.pallas_call(
        _outproj_kernel,
        grid=(n // tm,),
        in_specs=[pl.BlockSpec((tm, SB_WIDTH), row), pl.BlockSpec((tm, DF_WIDTH), row),
                  pl.BlockSpec((tm, d), row), pl.BlockSpec((1, SB_WIDTH), const),
                  pl.BlockSpec((SB_WIDTH, d), const), pl.BlockSpec((DF_WIDTH, d), const),
                  pl.BlockSpec((1, d), const), pl.BlockSpec((3 * d, LANES), const),
                  pl.BlockSpec((1, LANES), const)],
        out_specs=[pl.BlockSpec((tm, d), row), pl.BlockSpec((tm, d), row),
                   pl.BlockSpec((tm, LANES), row), pl.BlockSpec((META_ROWS, LANES), row)],
        out_shape=[jax.ShapeDtypeStruct((n, d), F32), jax.ShapeDtypeStruct((n, d), BF16),
                   jax.ShapeDtypeStruct((n, LANES), F32),
                   jax.ShapeDtypeStruct((n // tm * META_ROWS, LANES), F32)],
        compiler_params=_params("parallel"),
        name="outproj_router",
    )(o_sb, o_df, x2d, g_sb, w_sb, w_df, g_ffn, w_router3, b_router)


RUN_SIZES = (512, 256, 128, 64, 32, 16, 8)


def _lane_prefix(x_groups):
    r = lax.broadcasted_iota(jnp.int32, (LANES, LANES), 0)
    c = lax.broadcasted_iota(jnp.int32, (LANES, LANES), 1)
    return _dot(x_groups.astype(BF16), jnp.where(r < c, 1.0, 0.0).astype(BF16))


def _plan_kernel(route_ref, runs_ref, pos_ref, meta_ref, seg_ref, post_ref, run_ref, off_ref, *, tm, tile_rows):
    i = pl.program_id(0)
    route = route_ref[...]
    lane = lax.broadcasted_iota(jnp.int32, (tm, LANES), 1)
    lane_f = lane.astype(F32)
    sel1 = lane_f == route[:, 0:1]
    sel2 = lane_f == route[:, 1:2]
    run_rows, onehot = _run_rows(route)
    mrow = lax.broadcasted_iota(jnp.int32, (META_ROWS, LANES), 0)

    @pl.when(i == 0)
    def _():
        cnt = jnp.sum(runs_ref[...], axis=0, keepdims=True) * (1.0 / META_ROWS)
        cnt = jnp.broadcast_to(cnt, (META_ROWS, LANES))
        n_tiles = jnp.ceil(cnt * (1.0 / tile_rows))
        tile_off = _lane_prefix(n_tiles)
        off_ref[...] = tile_off * tile_rows
        run_ref[...] = jnp.zeros_like(run_ref)
        meta_ref[...] = jnp.where(mrow == 0, tile_off, jnp.where(mrow == 1, n_tiles,
                                                                 jnp.where(mrow == 2, cnt, 0.0)))

    r = lax.broadcasted_iota(jnp.int32, (tm, tm), 0)
    c = lax.broadcasted_iota(jnp.int32, (tm, tm), 1)
    earlier = _dot(jnp.where(c < r, 1.0, 0.0).astype(BF16), onehot.astype(BF16))
    local_start = _lane_prefix(jnp.broadcast_to(run_rows * (1.0 / ROW_GROUP), (META_ROWS, LANES))) * ROW_GROUP
    global_start = run_ref[...] + off_ref[...]
    base_g = earlier + global_start[0:1]
    base_l = earlier + local_start[0:1]

    def pick(sel, base):
        return jnp.sum(jnp.where(sel, base, 0.0), axis=-1, keepdims=True)

    pos = jnp.where(lane == 0, pick(sel1, base_g), jnp.where(
        lane == 1, pick(sel2, base_g), jnp.where(
            lane == 2, pick(sel1, base_l), jnp.where(lane == 3, pick(sel2, base_l), 0.0))))
    pos_ref[...] = pos.astype(jnp.int32)
    post_ref[...] = jnp.transpose(pos)[:META_ROWS].astype(jnp.int32)
    seg_ref[...] = jnp.where(mrow == 0, local_start, jnp.where(
        mrow == 1, run_rows, jnp.where(mrow == 2, global_start, 0.0))).astype(jnp.int32)
    run_ref[...] += run_rows


def _plan(route, runs, tm, tile_rows):
    n = route.shape[0]
    steps = n // tm
    assert (n + (ROW_GROUP - 1) * steps) // tile_rows + 1 <= 256, "per-expert tile counts must stay exact in bf16"
    assert (2 * tm) // ROW_GROUP + N_EXPERTS <= 256, "per-tile run sizes must stay exact in bf16"
    small = pltpu.VMEM((META_ROWS, LANES), F32)
    tile = lambda i: (i, 0)
    return pl.pallas_call(
        functools.partial(_plan_kernel, tm=tm, tile_rows=tile_rows),
        grid=(steps,),
        in_specs=[pl.BlockSpec((tm, LANES), tile),
                  pl.BlockSpec((steps * META_ROWS, LANES), lambda i: (0, 0))],
        out_specs=[pl.BlockSpec((tm, LANES), tile), pl.BlockSpec((META_ROWS, LANES), lambda i: (0, 0)),
                   pl.BlockSpec((META_ROWS, LANES), tile), pl.BlockSpec((META_ROWS, tm), tile)],
        out_shape=[jax.ShapeDtypeStruct((n, LANES), jnp.int32),
                   jax.ShapeDtypeStruct((META_ROWS, LANES), F32),
                   jax.ShapeDtypeStruct((steps * META_ROWS, LANES), jnp.int32),
                   jax.ShapeDtypeStruct((steps * META_ROWS, tm), jnp.int32)],
        scratch_shapes=[small, small],
        compiler_params=_params("arbitrary"),
        name="moe_plan",
    )(route, runs)


def _row_copy(src_ref, src_row, dst_ref, dst_row, sem):
    return pltpu.make_async_copy(src_ref.at[pl.ds(src_row, 1)], dst_ref.at[pl.ds(dst_row, 1)], sem)


def _for_each_run_piece(seg_ref, fn):
    def per_expert(e, carry):
        local = seg_ref[0, 0, e]
        length = seg_ref[0, 1, e]
        dst = seg_ref[0, 2, e]
        done = jnp.int32(0)
        for size in RUN_SIZES:
            has = (length & size) != 0

            @pl.when(has)
            def _(size=size, done=done):
                fn(pl.multiple_of(local + done, ROW_GROUP), pl.multiple_of(dst + done, ROW_GROUP), size)

            done = done + jnp.where(has, size, 0)
        return carry

    lax.fori_loop(0, N_EXPERTS, per_expert, 0)


def _dispatch_kernel(meta_ref, h_ref, pos_ref, seg_ref, seg_prev_ref, xs_ref, local_ref, zero_ref, sem, fill_sem,
                     *, tm, tile_rows):
    i = pl.program_id(0)
    last = pl.num_programs(0) - 1
    buf = lax.rem(i, 2)
    rows = local_ref.shape[1]
    local_pos = jnp.transpose(pos_ref[...].astype(F32))
    lp1 = local_pos[2:3].astype(jnp.int32)
    lp2 = local_pos[3:4].astype(jnp.int32)
    r = lax.broadcasted_iota(jnp.int32, (rows, tm), 0)
    place = jnp.where(r == lp1, 1.0, jnp.where(r == lp2, 1.0, 0.0)).astype(BF16)
    local_ref[buf] = _dot(place, h_ref[...])

    def run_copy(b):
        def make(local, dst, size):
            return pltpu.make_async_copy(local_ref.at[b, pl.ds(local, size)], xs_ref.at[pl.ds(dst, size)],
                                         sem.at[b])
        return make

    _for_each_run_piece(seg_ref, lambda local, dst, size: run_copy(buf)(local, dst, size).start())

    @pl.when(i > 0)
    def _():
        _for_each_run_piece(seg_prev_ref, lambda local, dst, size: run_copy(1 - buf)(local, dst, size).wait())

    @pl.when(i == last)
    def _():
        _for_each_run_piece(seg_ref, lambda local, dst, size: run_copy(buf)(local, dst, size).wait())
        zero_ref[...] = jnp.zeros_like(zero_ref)

        def per_expert(e, carry):
            off = meta_ref[0, e] * tile_rows
            lo = meta_ref[2, e]
            hi = meta_ref[1, e] * tile_rows

            def fill(g, c):
                pltpu.make_async_copy(zero_ref.at[pl.ds(0, ROW_GROUP)],
                                      xs_ref.at[pl.ds(pl.multiple_of(off + g * ROW_GROUP, ROW_GROUP), ROW_GROUP)],
                                      fill_sem).start()
                return c

            def drain(g, c):
                pltpu.make_async_copy(zero_ref.at[pl.ds(0, ROW_GROUP)], xs_ref.at[pl.ds(0, ROW_GROUP)],
                                      fill_sem).wait()
                return c

            lax.fori_loop(lo // ROW_GROUP, hi // ROW_GROUP, fill, 0)
            lax.fori_loop(lo // ROW_GROUP, hi // ROW_GROUP, drain, 0)
            return carry

        lax.fori_loop(0, N_EXPERTS, per_expert, 0)

        def spare_tile(j, carry):
            c = pltpu.make_async_copy(zero_ref, xs_ref.at[pl.ds(j * tile_rows, tile_rows)], fill_sem)
            c.start()
            c.wait()
            return carry

        n_used = meta_ref[0, N_EXPERTS - 1] + meta_ref[1, N_EXPERTS - 1]
        lax.fori_loop(n_used, xs_ref.shape[0] // tile_rows, spare_tile, 0)


def _dispatch(meta_i, h2, pos, seg3, n_rows, tm, tile_rows):
    n, d = h2.shape
    local_rows = 2 * tm + N_EXPERTS * ROW_GROUP
    seg_spec = lambda index_map: pl.BlockSpec((1, META_ROWS, LANES), index_map, memory_space=pltpu.SMEM)
    return pl.pallas_call(
        functools.partial(_dispatch_kernel, tm=tm, tile_rows=tile_rows),
        grid_spec=pltpu.PrefetchScalarGridSpec(
            num_scalar_prefetch=1,
            grid=(n // tm,),
            in_specs=[pl.BlockSpec((tm, d), lambda i, meta: (i, 0)),
                      pl.BlockSpec((tm, LANES), lambda i, meta: (i, 0)),
                      seg_spec(lambda i, meta: (i, 0, 0)),
                      seg_spec(lambda i, meta: (jnp.maximum(i - 1, 0), 0, 0))],
            out_specs=pl.BlockSpec(memory_space=pl.ANY),
            scratch_shapes=[pltpu.VMEM((2, local_rows, d), F32), pltpu.VMEM((tile_rows, d), F32),
                            pltpu.SemaphoreType.DMA((2,)), pltpu.SemaphoreType.DMA(())]),
        out_shape=jax.ShapeDtypeStruct((n_rows, d), F32),
        compiler_params=_params("arbitrary"),
        name="moe_dispatch",
    )(meta_i, h2, pos, seg3, seg3)


def _gmm_kernel(te_ref, nv_ref, x_ref, wg_ref, wu_ref, wd_ref, y_ref, wg_bf, wu_bf, wd_bf):
    j = pl.program_id(0)
    used = j < nv_ref[0]
    new_expert = jnp.logical_or(j == 0, te_ref[j] != te_ref[jnp.maximum(j - 1, 0)])

    @pl.when(jnp.logical_and(used, new_expert))
    def _():
        wg_bf[...] = wg_ref[0].astype(BF16)
        wu_bf[...] = wu_ref[0].astype(BF16)
        wd_bf[...] = wd_ref[0].astype(BF16)

    @pl.when(used)
    def _():
        x = x_ref[...].astype(BF16)
        a = _dot(x, wg_bf[...])
        hid = (a * jax.nn.sigmoid(a)) * _dot(x, wu_bf[...])
        y_ref[...] = _dot(hid.astype(BF16), wd_bf[...])

    @pl.when(jnp.logical_not(used))
    def _():
        y_ref[...] = jnp.zeros_like(y_ref)


def _gmm(tile_expert, n_valid, xs, wg, wu, wd, tile_rows):
    n_rows, d = xs.shape
    de = wg.shape[2]
    rows = lambda j, te, nv: (jnp.minimum(j, nv[0] - 1), 0)
    return pl.pallas_call(
        _gmm_kernel,
        grid_spec=pltpu.PrefetchScalarGridSpec(
            num_scalar_prefetch=2,
            grid=(n_rows // tile_rows,),
            in_specs=[pl.BlockSpec((tile_rows, d), rows),
                      pl.BlockSpec((1, d, de), lambda j, te, nv: (te[j], 0, 0)),
                      pl.BlockSpec((1, d, de), lambda j, te, nv: (te[j], 0, 0)),
                      pl.BlockSpec((1, de, d), lambda j, te, nv: (te[j], 0, 0))],
            out_specs=pl.BlockSpec((tile_rows, d), lambda j, te, nv: (j, 0)),
            scratch_shapes=[pltpu.VMEM((d, de), BF16), pltpu.VMEM((d, de), BF16), pltpu.VMEM((de, d), BF16)]),
        out_shape=jax.ShapeDtypeStruct((n_rows, d), F32),
        compiler_params=_params("arbitrary"),
        name="moe_experts",
    )(tile_expert, n_valid, xs, wg, wu, wd)


def _ple_kernel(x1_ref, p_ref, route_ref, pos0_ref, pos1_ref, pos0_next_ref, pos1_next_ref, ys_ref, gple_ref,
                wgate_ref, wproj_ref, gfin_ref, o_ref, y_ref, sem, *, tm):
    i = pl.program_id(0)
    buf = lax.rem(i, 2)

    def gather(pos, b):
        def start(g, carry):
            base = pl.multiple_of(g * ROW_GROUP, ROW_GROUP)
            for j in range(ROW_GROUP):
                for slot in range(2):
                    src = pos[slot][0, 0, base + j]
                    _row_copy(ys_ref, src, y_ref.at[b, slot], base + j, sem.at[b]).start(priority=slot)
            return carry

        lax.fori_loop(0, tm // ROW_GROUP, start, 0)

    @pl.when(i == 0)
    def _():
        gather((pos0_ref, pos1_ref), 0)

    @pl.when(i + 1 < pl.num_programs(0))
    def _():
        gather((pos0_next_ref, pos1_next_ref), 1 - buf)

    def wait(r, carry):
        _row_copy(ys_ref, 0, y_ref.at[buf, 0], 0, sem.at[buf]).wait()
        return carry

    emb = _dot(p_ref[...].astype(BF16), wproj_ref[...])
    lax.fori_loop(0, 2 * tm, wait, 0, unroll=8)
    route = route_ref[...]
    x = x1_ref[...] + (route[:, 2:3] * y_ref[buf, 0] + route[:, 3:4] * y_ref[buf, 1])
    gate = jax.nn.sigmoid(_dot(_rms(x, gple_ref[...]).astype(BF16), wgate_ref[...]))
    o_ref[...] = _rms(x + gate * emb, gfin_ref[...])


def _ple(x1, p2d, route, pos_slots, ys, g_ple, w_gate, w_proj, g_final, tm):
    n, d = x1.shape
    pd = p2d.shape[1]
    steps = n // tm
    row = lambda i: (i, 0)
    const = lambda i: (0, 0)
    pos_spec = lambda index_map: pl.BlockSpec((1, 1, tm), index_map, memory_space=pltpu.SMEM)
    here = lambda i: (i, 0, 0)
    ahead = lambda i: (jnp.minimum(i + 1, steps - 1), 0, 0)
    return pl.pallas_call(
        functools.partial(_ple_kernel, tm=tm),
        grid=(steps,),
        in_specs=[pl.BlockSpec((tm, d), row), pl.BlockSpec((tm, pd), row), pl.BlockSpec((tm, LANES), row),
                  pos_spec(here), pos_spec(here), pos_spec(ahead), pos_spec(ahead),
                  pl.BlockSpec(memory_space=pl.ANY), pl.BlockSpec((1, d), const),
                  pl.BlockSpec((d, d), const), pl.BlockSpec((pd, d), const), pl.BlockSpec((1, d), const)],
        out_specs=pl.BlockSpec((tm, d), row),
        out_shape=jax.ShapeDtypeStruct((n, d), F32),
        scratch_shapes=[pltpu.VMEM((2, 2, tm, d), F32), pltpu.SemaphoreType.DMA((2,))],
        compiler_params=_params("arbitrary"),
        name="combine_ple_final",
    )(x1, p2d, route, *pos_slots, *pos_slots, ys, g_ple, w_gate, w_proj, g_final)


def _layer(x, p_i, layer_idx, g_mix, w_in, lambda_q1, lambda_k1, lambda_q2, lambda_k2, g_sb_out, g_df_out,
           w_out, g_ffn, w_router_group, b_router_group, w_router_expert, b_router_expert, w_expert_gate,
           w_expert_up, w_expert_down, g_ple, w_ple_gate, w_ple_proj, g_out):
    b, s, d = x.shape
    n = b * s
    tm = min(512, n)
    t = min(256, s)
    x2d = x.reshape(n, d)
    lambda_init = 0.8 - 0.6 * math.exp(-0.3 * layer_idx)

    proj = _inproj(x2d, g_mix.reshape(1, d), w_in.astype(BF16), tm)
    proj3 = proj.reshape(b, s, proj.shape[1])

    tri = (lax.broadcasted_iota(jnp.int32, (t, t), 0) > lax.broadcasted_iota(jnp.int32, (t, t), 1))
    u2 = jnp.concatenate([tri, tri], axis=0).astype(BF16)
    o_sb = _sb_attention(proj3, u2, t).reshape(n, SB_WIDTH)

    lam_rows = jnp.stack([lambda_q1, lambda_k1, lambda_q2, lambda_k2]).astype(F32)
    o_df = _df_attention(proj3, lam_rows, g_df_out.reshape(1, LANES).astype(F32), min(512, s),
                         lambda_init).reshape(n, DF_WIDTH)

    w_router = jnp.concatenate([w_router_group, w_router_expert], axis=1).astype(F32)
    w_router = jnp.pad(w_router, ((0, 0), (0, LANES - w_router.shape[1])))
    wr_hi, wr_lo = _split_bf16(w_router)
    b_router = jnp.pad(jnp.concatenate([b_router_group, b_router_expert]).astype(F32),
                       (0, LANES - N_GROUPS - N_EXPERTS)).reshape(1, LANES)
    w_out_bf = w_out.astype(BF16)
    x1, h2, route, runs = _outproj(o_sb, o_df, x2d, g_sb_out.reshape(1, SB_WIDTH), w_out_bf[:SB_WIDTH],
                             w_out_bf[SB_WIDTH:], g_ffn.reshape(1, d),
                             jnp.concatenate([wr_hi, wr_lo, wr_hi], axis=0), b_router, tm)

    tile_rows = min(MOE_TILE_ROWS, n)
    pos, meta, seg, pos_t = _plan(route, runs, tm, tile_rows)
    run_padding = (n // tm) * N_EXPERTS * ROW_GROUP
    n_tiles = pl.cdiv(2 * n + run_padding, tile_rows) + N_EXPERTS
    meta_i = meta[:3, :N_EXPERTS].astype(jnp.int32)
    ends = meta_i[0] + meta_i[1]
    n_valid = ends[-1:]
    tile_ids = jnp.arange(n_tiles, dtype=jnp.int32)
    tile_expert = jnp.sum(tile_ids[:, None] >= ends[None, :], axis=1).astype(jnp.int32)
    tile_expert = jnp.where(tile_ids < n_valid, tile_expert, tile_expert[n_valid[0] - 1])
    xs = _dispatch(meta_i, h2, pos, seg.reshape(n // tm, META_ROWS, LANES), n_tiles * tile_rows, tm, tile_rows)
    ys = _gmm(tile_expert, n_valid, xs, w_expert_gate, w_expert_up, w_expert_down, tile_rows)

    pos_t3 = pos_t.reshape(n // tm, META_ROWS, tm)
    out = _ple(x1, p_i.reshape(n, p_i.shape[-1]), route, (pos_t3[:, 0:1], pos_t3[:, 1:2]), ys,
               g_ple.reshape(1, d), w_ple_gate.astype(BF16), w_ple_proj.astype(BF16), g_out.reshape(1, d), tm)
    return out.reshape(b, s, d)


def kernel(x, p, g_mix, w_in, lambda_q1, lambda_k1, lambda_q2, lambda_k2, g_sb_out, g_df_out, w_out, g_ffn,
           w_router_group, b_router_group, w_router_expert, b_router_expert, w_expert_gate, w_expert_up,
           w_expert_down, g_ple, w_ple_gate, w_ple_proj, g_final):
    depth = p.shape[0]
    assert depth == 1, "the final norm is fused into the single layer's last kernel"
    return _layer(x, p[0], 0, g_mix[0], w_in[0], lambda_q1[0], lambda_k1[0], lambda_q2[0], lambda_k2[0],
                  g_sb_out[0], g_df_out[0], w_out[0], g_ffn[0], w_router_group[0], b_router_group[0],
                  w_router_expert[0], b_router_expert[0], w_expert_gate[0], w_expert_up[0],
                  w_expert_down[0], g_ple[0], w_ple_gate[0], w_ple_proj[0], g_final)
```

```python
import functools
import math

import jax
import jax.numpy as jnp
from jax import lax
from jax.experimental import pallas as pl
from jax.experimental.pallas import tpu as pltpu

F32 = jnp.float32
BF16 = jnp.bfloat16

HEAD_DIM = 64
LANES = 128
N_SB_PAIRS = 4
N_DF_HEADS = 4
SB_WIDTH = 512
DF_WIDTH = 512
SCALE = HEAD_DIM ** -0.5
NORM_EPS = 1e-6
N_GROUPS = 4
EXPERTS_PER_GROUP = 4
N_EXPERTS = 16
ROUTER_LANE0 = N_GROUPS
ALIBI_SLOPES = tuple(2.0 ** (-8.0 * (h + 1) / N_DF_HEADS) for h in range(N_DF_HEADS))
SB_LOG_ZERO = -110.0
VMEM_LIMIT = 48 * 1024 * 1024
MOE_TILE_ROWS = 512
META_ROWS = 8
ROW_GROUP = 8


def _rms(x, g):
    ms = jnp.mean(x * x, axis=-1, keepdims=True)
    return x * lax.rsqrt(ms + NORM_EPS) * g


def _dot(a, b):
    return jnp.dot(a, b, preferred_element_type=F32)


def _dot_nt(a, b):
    return lax.dot_general(a, b, (((1,), (1,)), ((), ())), preferred_element_type=F32)


def _rows_dot(dot, a, b, rows):
    return jnp.concatenate([dot(a[r:r + rows], b) for r in range(0, a.shape[0], rows)], axis=0)


def _params(*sem):
    return pltpu.CompilerParams(dimension_semantics=sem, vmem_limit_bytes=VMEM_LIMIT)


def _inproj_kernel(x_ref, g_ref, w_ref, o_ref, *, tn):
    h = _rms(x_ref[...], g_ref[...]).astype(BF16)
    for j in range(o_ref.shape[1] // tn):
        o_ref[:, j * tn:(j + 1) * tn] = _dot(h, w_ref[:, j * tn:(j + 1) * tn]).astype(o_ref.dtype)


def _inproj(x2d, g, w_bf16, tm):
    n, d = x2d.shape
    width = w_bf16.shape[1]
    return pl.pallas_call(
        functools.partial(_inproj_kernel, tn=1024),
        grid=(n // tm,),
        in_specs=[pl.BlockSpec((tm, d), lambda i: (i, 0)),
                  pl.BlockSpec((1, d), lambda i: (0, 0)),
                  pl.BlockSpec((d, width), lambda i: (0, 0))],
        out_specs=pl.BlockSpec((tm, width), lambda i: (i, 0)),
        out_shape=jax.ShapeDtypeStruct((n, width), BF16),
        compiler_params=_params("parallel"),
        name="inproj",
    )(x2d, g, w_bf16)


SB_STREAMS = 4


def _sb_kernel(q_ref, k_ref, v_ref, u_ref, o_ref, acc_ref, c_ref, z_ref, *, t):
    qi = pl.program_id(2)
    lane = lax.broadcasted_iota(jnp.int32, (t, LANES), 1)
    first = lane < HEAD_DIM

    def stacked_q(p):
        q = q_ref[0, :, p * LANES:(p + 1) * LANES] * SCALE
        zero = jnp.zeros_like(q)
        return jnp.concatenate([jnp.where(first, q, zero), jnp.where(first, zero, q)], axis=0)

    q2 = [stacked_q(p) for p in range(SB_STREAMS)]

    def logits(p, kb):
        k = k_ref[0, pl.ds(pl.multiple_of(kb * t, t), t), p * LANES:(p + 1) * LANES]
        return _rows_dot(_dot_nt, q2[p], k, t)

    def consume(p, z, kb, strict_mask):
        v = v_ref[0, pl.ds(pl.multiple_of(kb * t, t), t), p * LANES:(p + 1) * LANES]
        soft = jnp.log(1.0 + jnp.exp(-jnp.abs(z)))
        log_beta = jnp.minimum(z, 0.0) - soft
        log_keep = log_beta - z
        if strict_mask is not None:
            log_keep = jnp.where(strict_mask, log_keep, 0.0)
        hi = log_keep.astype(BF16)
        lo = (log_keep - hi.astype(F32)).astype(BF16)
        rev = _rows_dot(_dot, jnp.concatenate([hi, lo], axis=1), u_ref[...], t)
        c = c_ref[p]
        w = jnp.exp(log_beta + rev + jnp.concatenate([c] * (t // LANES), axis=1))
        if strict_mask is not None:
            w = jnp.where(strict_mask, w, 0.0)
        w = w.astype(BF16)
        vz = jnp.zeros_like(v)
        v2 = jnp.concatenate([jnp.where(first, v, vz), jnp.where(first, vz, v)], axis=0)
        acc_ref[p] += _dot(jnp.concatenate([w[:t], w[t:]], axis=1), v2)
        c_new = c + jnp.sum(log_keep, axis=-1, keepdims=True)
        c_ref[p] = c_new
        return jnp.max(c_new)

    def step(kb, strict_mask):
        kb_next = jnp.maximum(kb - 1, 0)
        worst = None
        for p in range(SB_STREAMS):
            z = z_ref[p]
            z_ref[p] = logits(p, kb_next)
            done = consume(p, z, kb, strict_mask)
            worst = done if worst is None else jnp.maximum(worst, done)
        return worst

    acc_ref[...] = jnp.zeros_like(acc_ref)
    c_ref[...] = jnp.zeros_like(c_ref)
    for p in range(SB_STREAMS):
        z_ref[p] = logits(p, qi)
    row = lax.broadcasted_iota(jnp.int32, (2 * t, t), 0)
    col = lax.broadcasted_iota(jnp.int32, (2 * t, t), 1)
    cmax = step(qi, col < jnp.where(row >= t, row - t, row))

    def cond(carry):
        kb, cm = carry
        return jnp.logical_and(kb >= 0, cm > SB_LOG_ZERO)

    def body(carry):
        kb, _ = carry
        return kb - 1, step(kb, None)

    lax.while_loop(cond, body, (qi - 1, cmax))
    for p in range(SB_STREAMS):
        o_ref[0, :, p * LANES:(p + 1) * LANES] = acc_ref[p]


def _sb_attention(proj3, u2, t):
    b, s, _ = proj3.shape
    groups = N_SB_PAIRS // SB_STREAMS
    width = SB_STREAMS * LANES
    return pl.pallas_call(
        functools.partial(_sb_kernel, t=t),
        grid=(b, groups, s // t),
        in_specs=[pl.BlockSpec((1, t, width), lambda bi, g, qi: (bi, qi, g)),
                  pl.BlockSpec((1, s, width), lambda bi, g, qi: (bi, 0, groups + g)),
                  pl.BlockSpec((1, s, width), lambda bi, g, qi: (bi, 0, 2 * groups + g)),
                  pl.BlockSpec((2 * t, t), lambda bi, g, qi: (0, 0))],
        out_specs=pl.BlockSpec((1, t, width), lambda bi, g, qi: (bi, qi, g)),
        out_shape=jax.ShapeDtypeStruct((b, s, SB_WIDTH), F32),
        scratch_shapes=[pltpu.VMEM((SB_STREAMS, t, LANES), F32), pltpu.VMEM((SB_STREAMS, 2 * t, LANES), F32),
                        pltpu.VMEM((SB_STREAMS, 2 * t, t), F32)],
        compiler_params=_params("parallel", "parallel", "arbitrary"),
        name="sb_attention",
    )(proj3, proj3, proj3, u2)


DF_STREAMS = 2


def _df_kernel(q_ref, k_ref, v_ref, lam_ref, g_ref, o_ref, acc_ref, m_ref, z_ref, *, t, lambda_init):
    for p in range(DF_STREAMS):
        _df_head(p, q_ref, k_ref, v_ref, lam_ref, g_ref, o_ref, acc_ref, m_ref, z_ref, t, lambda_init)


def _df_head(p, q_ref, k_ref, v_ref, lam_ref, g_ref, o_ref, acc_ref, m_ref, z_ref, t, lambda_init):
    h = pl.program_id(1) * DF_STREAMS + p
    lanes = slice(p * LANES, (p + 1) * LANES)
    qi = pl.program_id(2)
    slope = jnp.float32(ALIBI_SLOPES[-1])
    for idx in range(N_DF_HEADS - 2, -1, -1):
        slope = jnp.where(h == idx, jnp.float32(ALIBI_SLOPES[idx]), slope)
    lane = lax.broadcasted_iota(jnp.int32, (t, LANES), 1)
    row = lax.broadcasted_iota(jnp.int32, (t, LANES), 0)
    first = lane < HEAD_DIM
    q = q_ref[0, :, lanes] * SCALE
    zero = jnp.zeros_like(q)
    bias_on = jnp.where(lane < 2, 1.0, 0.0).astype(BF16)
    q2 = jnp.concatenate([jnp.concatenate([jnp.where(first, q, zero), bias_on], axis=1),
                          jnp.concatenate([jnp.where(first, zero, q), bias_on], axis=1)], axis=0)
    key_lo = jnp.where(lane == 0, ((row >> 8) << 8).astype(F32),
                       jnp.where(lane == 1, (row & 255).astype(F32), 0.0)) * slope
    lane0 = lane == 0
    ones_v = jnp.ones((t, LANES), BF16)

    def logits(kb):
        k = k_ref[0, pl.ds(pl.multiple_of(kb * t, t), t), lanes]
        offset = slope * ((kb - qi) * t).astype(F32)
        k_bias = (key_lo + jnp.where(lane0, offset, 0.0)).astype(BF16)
        return _dot_nt(q2, jnp.concatenate([k, k_bias], axis=1))

    def accumulate(z, kb):
        v = v_ref[0, pl.ds(pl.multiple_of(kb * t, t), t), lanes]
        m_old = m_ref[...]
        m_new = jnp.maximum(m_old, jnp.max(z, axis=-1, keepdims=True))
        alpha = jnp.exp(m_old - m_new)
        p = jnp.exp(z - jnp.concatenate([m_new] * (t // LANES), axis=1)).astype(BF16)
        pv = _dot(p, jnp.concatenate([v, ones_v], axis=1))
        acc_ref[...] = jnp.concatenate([alpha, alpha], axis=1) * acc_ref[...] + pv
        m_ref[...] = m_new

    acc_ref[...] = jnp.zeros_like(acc_ref)
    m_ref[...] = jnp.full_like(m_ref, -jnp.inf)
    r2 = lax.broadcasted_iota(jnp.int32, (2 * t, t), 0)
    c2 = lax.broadcasted_iota(jnp.int32, (2 * t, t), 1)
    z_ref[...] = jnp.where(c2 <= jnp.where(r2 >= t, r2 - t, r2), logits(qi), -jnp.inf)

    def body(j, carry):
        z = z_ref[...]
        z_ref[...] = logits(j)
        accumulate(z, jnp.where(j == 0, qi, j - 1))
        return carry

    lax.fori_loop(0, qi, body, 0)
    accumulate(z_ref[...], jnp.where(qi == 0, qi, qi - 1))

    lam_vec = lam_ref[...]
    lam = (jnp.exp(jnp.sum(lam_vec[0:1] * lam_vec[1:2], axis=-1, keepdims=True))
           - jnp.exp(jnp.sum(lam_vec[2:3] * lam_vec[3:4], axis=-1, keepdims=True))
           + lambda_init)
    acc = acc_ref[...]
    ratio = acc[:, :LANES] / acc[:, LANES:]
    o = ratio[:t] - lam * ratio[t:]
    o_ref[0, :, lanes] = (_rms(o, g_ref[...]) * (1.0 - lambda_init)).astype(o_ref.dtype)


def _df_attention(proj3, lam_rows, g_df, t, lambda_init):
    b, s, _ = proj3.shape
    groups = N_DF_HEADS // DF_STREAMS
    width = DF_STREAMS * LANES
    col0 = 3 * N_SB_PAIRS // DF_STREAMS
    return pl.pallas_call(
        functools.partial(_df_kernel, t=t, lambda_init=lambda_init),
        grid=(b, groups, s // t),
        in_specs=[pl.BlockSpec((1, t, width), lambda bi, g, qi: (bi, qi, col0 + g)),
                  pl.BlockSpec((1, s, width), lambda bi, g, qi: (bi, 0, col0 + groups + g)),
                  pl.BlockSpec((1, s, width), lambda bi, g, qi: (bi, 0, col0 + 2 * groups + g)),
                  pl.BlockSpec((4, HEAD_DIM), lambda bi, g, qi: (0, 0)),
                  pl.BlockSpec((1, LANES), lambda bi, g, qi: (0, 0))],
        out_specs=pl.BlockSpec((1, t, width), lambda bi, g, qi: (bi, qi, g)),
        out_shape=jax.ShapeDtypeStruct((b, s, DF_WIDTH), BF16),
        scratch_shapes=[pltpu.VMEM((2 * t, 2 * LANES), F32), pltpu.VMEM((2 * t, LANES), F32),
                        pltpu.VMEM((2 * t, t), F32)],
        compiler_params=_params("parallel", "parallel", "arbitrary"),
        name="df_attention",
    )(proj3, proj3, proj3, lam_rows, g_df)


def _split_bf16(x):
    hi = x.astype(BF16)
    return hi, (x - hi.astype(F32)).astype(BF16)


def _route(logits):
    lane = lax.broadcasted_iota(jnp.int32, logits.shape, 1)
    lane_f = lane.astype(F32)
    neg = -jnp.inf
    big = float(LANES)
    gl = jnp.where(lane < N_GROUPS, logits, neg)
    gmax = jnp.max(gl, axis=-1, keepdims=True)
    g_idx = jnp.min(jnp.where(gl == gmax, lane_f, big), axis=-1, keepdims=True)
    g_w = 1.0 / jnp.sum(jnp.exp(gl - gmax), axis=-1, keepdims=True)
    e_lane = lane - ROUTER_LANE0
    in_group = jnp.logical_and(jnp.logical_and(e_lane >= 0, e_lane < N_EXPERTS),
                               (e_lane // EXPERTS_PER_GROUP).astype(F32) == g_idx)
    v1 = jnp.where(in_group, logits, neg)
    t1 = jnp.max(v1, axis=-1, keepdims=True)
    i1 = jnp.min(jnp.where(v1 == t1, lane_f, big), axis=-1, keepdims=True)
    v2 = jnp.where(lane_f == i1, neg, v1)
    t2 = jnp.max(v2, axis=-1, keepdims=True)
    i2 = jnp.min(jnp.where(v2 == t2, lane_f, big), axis=-1, keepdims=True)
    e2 = jnp.exp(t2 - t1)
    w1 = g_w / (1.0 + e2)
    w2 = w1 * e2
    return jnp.where(lane == 0, i1 - ROUTER_LANE0, jnp.where(
        lane == 1, i2 - ROUTER_LANE0, jnp.where(lane == 2, w1, jnp.where(lane == 3, w2, 0.0))))


def _run_rows(route):
    lane_f = lax.broadcasted_iota(jnp.int32, route.shape, 1).astype(F32)
    onehot = jnp.where(lane_f == route[:, 0:1], 1.0, jnp.where(lane_f == route[:, 1:2], 1.0, 0.0))
    return jnp.ceil(jnp.sum(onehot, axis=0, keepdims=True) * (1.0 / ROW_GROUP)) * ROW_GROUP, onehot


def _outproj_kernel(osb_ref, odf_ref, x_ref, gsb_ref, wsb_ref, wdf_ref, gffn_ref, wr_ref, br_ref,
                    x1_ref, h2_ref, route_ref, runs_ref):
    a_sb = _rms(osb_ref[...], gsb_ref[...]).astype(BF16)
    x1 = x_ref[...] + _dot(a_sb, wsb_ref[...]) + _dot(odf_ref[...], wdf_ref[...])
    x1_ref[...] = x1
    h2 = _rms(x1, gffn_ref[...])
    h2_ref[...] = h2.astype(BF16)
    hi, lo = _split_bf16(h2)
    logits = _rows_dot(_dot, jnp.concatenate([hi, hi, lo], axis=1), wr_ref[...], hi.shape[0] // 2) + br_ref[...]
    route = _route(logits)
    route_ref[...] = route
    runs_ref[...] = jnp.broadcast_to(_run_rows(route)[0], runs_ref.shape)


def _outproj(o_sb, o_df, x2d, g_sb, w_sb, w_df, g_ffn, w_router3, b_router, tm):
    n, d = x2d.shape
    row = lambda i: (i, 0)
    const = lambda i: (0, 0)
    return pl.pallas_call(
        _outproj_kernel,
        grid=(n // tm,),
        in_specs=[pl.BlockSpec((tm, SB_WIDTH), row), pl.BlockSpec((tm, DF_WIDTH), row),
                  pl.BlockSpec((tm, d), row), pl.BlockSpec((1, SB_WIDTH), const),
                  pl.BlockSpec((SB_WIDTH, d), const), pl.BlockSpec((DF_WIDTH, d), const),
                  pl.BlockSpec((1, d), const), pl.BlockSpec((3 * d, LANES), const),
                  pl.BlockSpec((1, LANES), const)],
        out_specs=[pl.BlockSpec((tm, d), row), pl.BlockSpec((tm, d), row),
                   pl.BlockSpec((tm, LANES), row), pl.BlockSpec((META_ROWS, LANES), row)],
        out_shape=[jax.ShapeDtypeStruct((n, d), F32), jax.ShapeDtypeStruct((n, d), BF16),
                   jax.ShapeDtypeStruct((n, LANES), F32),
                   jax.ShapeDtypeStruct((n // tm * META_ROWS, LANES), F32)],
        compiler_params=_params("parallel"),
        name="outproj_router",
    )(o_sb, o_df, x2d, g_sb, w_sb, w_df, g_ffn, w_router3, b_router)


RUN_SIZES = (512, 256, 128, 64, 32, 16, 8)


def _lane_prefix(x_groups):
    r = lax.broadcasted_iota(jnp.int32, (LANES, LANES), 0)
    c = lax.broadcasted_iota(jnp.int32, (LANES, LANES), 1)
    return _dot(x_groups.astype(BF16), jnp.where(r < c, 1.0, 0.0).astype(BF16))


def _plan_kernel(route_ref, runs_ref, pos_ref, meta_ref, seg_ref, post_ref, run_ref, off_ref, *, tm, tile_rows):
    i = pl.program_id(0)
    route = route_ref[...]
    lane = lax.broadcasted_iota(jnp.int32, (tm, LANES), 1)
    lane_f = lane.astype(F32)
    sel1 = lane_f == route[:, 0:1]
    sel2 = lane_f == route[:, 1:2]
    run_rows, onehot = _run_rows(route)
    mrow = lax.broadcasted_iota(jnp.int32, (META_ROWS, LANES), 0)

    @pl.when(i == 0)
    def _():
        cnt = jnp.sum(runs_ref[...], axis=0, keepdims=True) * (1.0 / META_ROWS)
        cnt = jnp.broadcast_to(cnt, (META_ROWS, LANES))
        n_tiles = jnp.ceil(cnt * (1.0 / tile_rows))
        tile_off = _lane_prefix(n_tiles)
        off_ref[...] = tile_off * tile_rows
        run_ref[...] = jnp.zeros_like(run_ref)
        meta_ref[...] = jnp.where(mrow == 0, tile_off, jnp.where(mrow == 1, n_tiles,
                                                                 jnp.where(mrow == 2, cnt, 0.0)))

    r = lax.broadcasted_iota(jnp.int32, (tm, tm), 0)
    c = lax.broadcasted_iota(jnp.int32, (tm, tm), 1)
    earlier = _dot(jnp.where(c < r, 1.0, 0.0).astype(BF16), onehot.astype(BF16))
    local_start = _lane_prefix(jnp.broadcast_to(run_rows * (1.0 / ROW_GROUP), (META_ROWS, LANES))) * ROW_GROUP
    global_start = run_ref[...] + off_ref[...]
    base_g = earlier + global_start[0:1]
    base_l = earlier + local_start[0:1]

    def pick(sel, base):
        return jnp.sum(jnp.where(sel, base, 0.0), axis=-1, keepdims=True)

    pos = jnp.where(lane == 0, pick(sel1, base_g), jnp.where(
        lane == 1, pick(sel2, base_g), jnp.where(
            lane == 2, pick(sel1, base_l), jnp.where(lane == 3, pick(sel2, base_l), 0.0))))
    pos_ref[...] = pos.astype(jnp.int32)
    post_ref[...] = jnp.transpose(pos)[:META_ROWS].astype(jnp.int32)
    seg_ref[...] = jnp.where(mrow == 0, local_start, jnp.where(
        mrow == 1, run_rows, jnp.where(mrow == 2, global_start, 0.0))).astype(jnp.int32)
    run_ref[...] += run_rows


def _plan(route, runs, tm, tile_rows):
    n = route.shape[0]
    steps = n // tm
    assert (n + (ROW_GROUP - 1) * steps) // tile_rows + 1 <= 256, "per-expert tile counts must stay exact in bf16"
    assert (2 * tm) // ROW_GROUP + N_EXPERTS <= 256, "per-tile run sizes must stay exact in bf16"
    small = pltpu.VMEM((META_ROWS, LANES), F32)
    tile = lambda i: (i, 0)
    return pl.pallas_call(
        functools.partial(_plan_kernel, tm=tm, tile_rows=tile_rows),
        grid=(steps,),
        in_specs=[pl.BlockSpec((tm, LANES), tile),
                  pl.BlockSpec((steps * META_ROWS, LANES), lambda i: (0, 0))],
        out_specs=[pl.BlockSpec((tm, LANES), tile), pl.BlockSpec((META_ROWS, LANES), lambda i: (0, 0)),
                   pl.BlockSpec((META_ROWS, LANES), tile), pl.BlockSpec((META_ROWS, tm), tile)],
        out_shape=[jax.ShapeDtypeStruct((n, LANES), jnp.int32),
                   jax.ShapeDtypeStruct((META_ROWS, LANES), F32),
                   jax.ShapeDtypeStruct((steps * META_ROWS, LANES), jnp.int32),
                   jax.ShapeDtypeStruct((steps * META_ROWS, tm), jnp.int32)],
        scratch_shapes=[small, small],
        compiler_params=_params("arbitrary"),
        name="moe_plan",
    )(route, runs)


def _row_copy(src_ref, src_row, dst_ref, dst_row, sem):
    return pltpu.make_async_copy(src_ref.at[pl.ds(src_row, 1)], dst_ref.at[pl.ds(dst_row, 1)], sem)


def _for_each_run_piece(seg_ref, fn):
    def per_expert(e, carry):
        local = seg_ref[0, 0, e]
        length = seg_ref[0, 1, e]
        dst = seg_ref[0, 2, e]
        done = jnp.int32(0)
        for size in RUN_SIZES:
            has = (length & size) != 0

            @pl.when(has)
            def _(size=size, done=done):
                fn(pl.multiple_of(local + done, ROW_GROUP), pl.multiple_of(dst + done, ROW_GROUP), size)

            done = done + jnp.where(has, size, 0)
        return carry

    lax.fori_loop(0, N_EXPERTS, per_expert, 0)


def _dispatch_kernel(meta_ref, h_ref, pos_ref, seg_ref, seg_prev_ref, xs_ref, local_ref, zero_ref, sem, fill_sem,
                     *, tm, tile_rows):
    i = pl.program_id(0)
    last = pl.num_programs(0) - 1
    buf = lax.rem(i, 2)
    rows = local_ref.shape[1]
    local_pos = jnp.transpose(pos_ref[...].astype(F32))
    lp1 = local_pos[2:3].astype(jnp.int32)
    lp2 = local_pos[3:4].astype(jnp.int32)
    r = lax.broadcasted_iota(jnp.int32, (rows, tm), 0)
    place = jnp.where(r == lp1, 1.0, jnp.where(r == lp2, 1.0, 0.0)).astype(BF16)
    local_ref[buf] = _dot(place, h_ref[...])

    def run_copy(b):
        def make(local, dst, size):
            return pltpu.make_async_copy(local_ref.at[b, pl.ds(local, size)], xs_ref.at[pl.ds(dst, size)],
                                         sem.at[b])
        return make

    _for_each_run_piece(seg_ref, lambda local, dst, size: run_copy(buf)(local, dst, size).start())

    @pl.when(i > 0)
    def _():
        _for_each_run_piece(seg_prev_ref, lambda local, dst, size: run_copy(1 - buf)(local, dst, size).wait())

    @pl.when(i == last)
    def _():
        _for_each_run_piece(seg_ref, lambda local, dst, size: run_copy(buf)(local, dst, size).wait())
        zero_ref[...] = jnp.zeros_like(zero_ref)

        def per_expert(e, carry):
            off = meta_ref[0, e] * tile_rows
            lo = meta_ref[2, e]
            hi = meta_ref[1, e] * tile_rows

            def fill(g, c):
                pltpu.make_async_copy(zero_ref.at[pl.ds(0, ROW_GROUP)],
                                      xs_ref.at[pl.ds(pl.multiple_of(off + g * ROW_GROUP, ROW_GROUP), ROW_GROUP)],
                                      fill_sem).start()
                return c

            def drain(g, c):
                pltpu.make_async_copy(zero_ref.at[pl.ds(0, ROW_GROUP)], xs_ref.at[pl.ds(0, ROW_GROUP)],
                                      fill_sem).wait()
                return c

            lax.fori_loop(lo // ROW_GROUP, hi // ROW_GROUP, fill, 0)
            lax.fori_loop(lo // ROW_GROUP, hi // ROW_GROUP, drain, 0)
            return carry

        lax.fori_loop(0, N_EXPERTS, per_expert, 0)

        def spare_tile(j, carry):
            c = pltpu.make_async_copy(zero_ref, xs_ref.at[pl.ds(j * tile_rows, tile_rows)], fill_sem)
            c.start()
            c.wait()
            return carry

        n_used = meta_ref[0, N_EXPERTS - 1] + meta_ref[1, N_EXPERTS - 1]
        lax.fori_loop(n_used, xs_ref.shape[0] // tile_rows, spare_tile, 0)


def _dispatch(meta_i, h2, pos, seg3, n_rows, tm, tile_rows):
    n, d = h2.shape
    local_rows = 2 * tm + N_EXPERTS * ROW_GROUP
    seg_spec = lambda index_map: pl.BlockSpec((1, META_ROWS, LANES), index_map, memory_space=pltpu.SMEM)
    return pl.pallas_call(
        functools.partial(_dispatch_kernel, tm=tm, tile_rows=tile_rows),
        grid_spec=pltpu.PrefetchScalarGridSpec(
            num_scalar_prefetch=1,
            grid=(n // tm,),
            in_specs=[pl.BlockSpec((tm, d), lambda i, meta: (i, 0)),
                      pl.BlockSpec((tm, LANES), lambda i, meta: (i, 0)),
                      seg_spec(lambda i, meta: (i, 0, 0)),
                      seg_spec(lambda i, meta: (jnp.maximum(i - 1, 0), 0, 0))],
            out_specs=pl.BlockSpec(memory_space=pl.ANY),
            scratch_shapes=[pltpu.VMEM((2, local_rows, d), F32), pltpu.VMEM((tile_rows, d), F32),
                            pltpu.SemaphoreType.DMA((2,)), pltpu.SemaphoreType.DMA(())]),
        out_shape=jax.ShapeDtypeStruct((n_rows, d), F32),
        compiler_params=_params("arbitrary"),
        name="moe_dispatch",
    )(meta_i, h2, pos, seg3, seg3)


def _gmm_kernel(te_ref, nv_ref, x_ref, wg_ref, wu_ref, wd_ref, y_ref, wg_bf, wu_bf, wd_bf):
    j = pl.program_id(0)
    used = j < nv_ref[0]
    new_expert = jnp.logical_or(j == 0, te_ref[j] != te_ref[jnp.maximum(j - 1, 0)])

    @pl.when(jnp.logical_and(used, new_expert))
    def _():
        wg_bf[...] = wg_ref[0].astype(BF16)
        wu_bf[...] = wu_ref[0].astype(BF16)
        wd_bf[...] = wd_ref[0].astype(BF16)

    @pl.when(used)
    def _():
        x = x_ref[...].astype(BF16)
        a = _dot(x, wg_bf[...])
        hid = (a * jax.nn.sigmoid(a)) * _dot(x, wu_bf[...])
        y_ref[...] = _dot(hid.astype(BF16), wd_bf[...])

    @pl.when(jnp.logical_not(used))
    def _():
        y_ref[...] = jnp.zeros_like(y_ref)


def _gmm(tile_expert, n_valid, xs, wg, wu, wd, tile_rows):
    n_rows, d = xs.shape
    de = wg.shape[2]
    rows = lambda j, te, nv: (jnp.minimum(j, nv[0] - 1), 0)
    return pl.pallas_call(
        _gmm_kernel,
        grid_spec=pltpu.PrefetchScalarGridSpec(
            num_scalar_prefetch=2,
            grid=(n_rows // tile_rows,),
            in_specs=[pl.BlockSpec((tile_rows, d), rows),
                      pl.BlockSpec((1, d, de), lambda j, te, nv: (te[j], 0, 0)),
                      pl.BlockSpec((1, d, de), lambda j, te, nv: (te[j], 0, 0)),
                      pl.BlockSpec((1, de, d), lambda j, te, nv: (te[j], 0, 0))],
            out_specs=pl.BlockSpec((tile_rows, d), lambda j, te, nv: (j, 0)),
            scratch_shapes=[pltpu.VMEM((d, de), BF16), pltpu.VMEM((d, de), BF16), pltpu.VMEM((de, d), BF16)]),
        out_shape=jax.ShapeDtypeStruct((n_rows, d), F32),
        compiler_params=_params("arbitrary"),
        name="moe_experts",
    )(tile_expert, n_valid, xs, wg, wu, wd)


def _ple_kernel(x1_ref, p_ref, route_ref, pos0_ref, pos1_ref, pos0_next_ref, pos1_next_ref, ys_ref, gple_ref,
                wgate_ref, wproj_ref, gfin_ref, o_ref, y_ref, sem, *, tm):
    i = pl.program_id(0)
    buf = lax.rem(i, 2)

    def gather(pos, b):
        def start(g, carry):
            base = pl.multiple_of(g * ROW_GROUP, ROW_GROUP)
            for j in range(ROW_GROUP):
                for slot in range(2):
                    src = pos[slot][0, 0, base + j]
                    _row_copy(ys_ref, src, y_ref.at[b, slot], base + j, sem.at[b]).start(priority=slot)
            return carry

        lax.fori_loop(0, tm // ROW_GROUP, start, 0)

    @pl.when(i == 0)
    def _():
        gather((pos0_ref, pos1_ref), 0)

    @pl.when(i + 1 < pl.num_programs(0))
    def _():
        gather((pos0_next_ref, pos1_next_ref), 1 - buf)

    def wait(r, carry):
        _row_copy(ys_ref, 0, y_ref.at[buf, 0], 0, sem.at[buf]).wait()
        return carry

    emb = _dot(p_ref[...].astype(BF16), wproj_ref[...])
    lax.fori_loop(0, 2 * tm, wait, 0, unroll=8)
    route = route_ref[...]
    x = x1_ref[...] + (route[:, 2:3] * y_ref[buf, 0] + route[:, 3:4] * y_ref[buf, 1])
    gate = jax.nn.sigmoid(_dot(_rms(x, gple_ref[...]).astype(BF16), wgate_ref[...]))
    o_ref[...] = _rms(x + gate * emb, gfin_ref[...])


def _ple(x1, p2d, route, pos_slots, ys, g_ple, w_gate, w_proj, g_final, tm):
    n, d = x1.shape
    pd = p2d.shape[1]
    steps = n // tm
    row = lambda i: (i, 0)
    const = lambda i: (0, 0)
    pos_spec = lambda index_map: pl.BlockSpec((1, 1, tm), index_map, memory_space=pltpu.SMEM)
    here = lambda i: (i, 0, 0)
    ahead = lambda i: (jnp.minimum(i + 1, steps - 1), 0, 0)
    return pl.pallas_call(
        functools.partial(_ple_kernel, tm=tm),
        grid=(steps,),
        in_specs=[pl.BlockSpec((tm, d), row), pl.BlockSpec((tm, pd), row), pl.BlockSpec((tm, LANES), row),
                  pos_spec(here), pos_spec(here), pos_spec(ahead), pos_spec(ahead),
                  pl.BlockSpec(memory_space=pl.ANY), pl.BlockSpec((1, d), const),
                  pl.BlockSpec((d, d), const), pl.BlockSpec((pd, d), const), pl.BlockSpec((1, d), const)],
        out_specs=pl.BlockSpec((tm, d), row),
        out_shape=jax.ShapeDtypeStruct((n, d), F32),
        scratch_shapes=[pltpu.VMEM((2, 2, tm, d), F32), pltpu.SemaphoreType.DMA((2,))],
        compiler_params=_params("arbitrary"),
        name="combine_ple_final",
    )(x1, p2d, route, *pos_slots, *pos_slots, ys, g_ple, w_gate, w_proj, g_final)


def _layer(x, p_i, layer_idx, g_mix, w_in, lambda_q1, lambda_k1, lambda_q2, lambda_k2, g_sb_out, g_df_out,
           w_out, g_ffn, w_router_group, b_router_group, w_router_expert, b_router_expert, w_expert_gate,
           w_expert_up, w_expert_down, g_ple, w_ple_gate, w_ple_proj, g_out):
    b, s, d = x.shape
    n = b * s
    tm = min(512, n)
    t = min(256, s)
    x2d = x.reshape(n, d)
    lambda_init = 0.8 - 0.6 * math.exp(-0.3 * layer_idx)

    proj = _inproj(x2d, g_mix.reshape(1, d), w_in.astype(BF16), tm)
    proj3 = proj.reshape(b, s, proj.shape[1])

    tri = (lax.broadcasted_iota(jnp.int32, (t, t), 0) > lax.broadcasted_iota(jnp.int32, (t, t), 1))
    u2 = jnp.concatenate([tri, tri], axis=0).astype(BF16)
    o_sb = _sb_attention(proj3, u2, t).reshape(n, SB_WIDTH)

    lam_rows = jnp.stack([lambda_q1, lambda_k1, lambda_q2, lambda_k2]).astype(F32)
    o_df = _df_attention(proj3, lam_rows, g_df_out.reshape(1, LANES).astype(F32), min(512, s),
                         lambda_init).reshape(n, DF_WIDTH)

    w_router = jnp.concatenate([w_router_group, w_router_expert], axis=1).astype(F32)
    w_router = jnp.pad(w_router, ((0, 0), (0, LANES - w_router.shape[1])))
    wr_hi, wr_lo = _split_bf16(w_router)
    b_router = jnp.pad(jnp.concatenate([b_router_group, b_router_expert]).astype(F32),
                       (0, LANES - N_GROUPS - N_EXPERTS)).reshape(1, LANES)
    w_out_bf = w_out.astype(BF16)
    x1, h2, route, runs = _outproj(o_sb, o_df, x2d, g_sb_out.reshape(1, SB_WIDTH), w_out_bf[:SB_WIDTH],
                             w_out_bf[SB_WIDTH:], g_ffn.reshape(1, d),
                             jnp.concatenate([wr_hi, wr_lo, wr_hi], axis=0), b_router, tm)

    tile_rows = min(MOE_TILE_ROWS, n)
    pos, meta, seg, pos_t = _plan(route, runs, tm, tile_rows)
    run_padding = (n // tm) * N_EXPERTS * ROW_GROUP
    n_tiles = pl.cdiv(2 * n + run_padding, tile_rows) + N_EXPERTS
    meta_i = meta[:3, :N_EXPERTS].astype(jnp.int32)
    ends = meta_i[0] + meta_i[1]
    n_valid = ends[-1:]
    tile_ids = jnp.arange(n_tiles, dtype=jnp.int32)
    tile_expert = jnp.sum(tile_ids[:, None] >= ends[None, :], axis=1).astype(jnp.int32)
    tile_expert = jnp.where(tile_ids < n_valid, tile_expert, tile_expert[n_valid[0] - 1])
    xs = _dispatch(meta_i, h2, pos, seg.reshape(n // tm, META_ROWS, LANES), n_tiles * tile_rows, tm, tile_rows)
    ys = _gmm(tile_expert, n_valid, xs, w_expert_gate, w_expert_up, w_expert_down, tile_rows)

    pos_t3 = pos_t.reshape(n // tm, META_ROWS, tm)
    out = _ple(x1, p_i.reshape(n, p_i.shape[-1]), route, (pos_t3[:, 0:1], pos_t3[:, 1:2]), ys,
               g_ple.reshape(1, d), w_ple_gate.astype(BF16), w_ple_proj.astype(BF16), g_out.reshape(1, d), tm)
    return out.reshape(b, s, d)


def kernel(x, p, g_mix, w_in, lambda_q1, lambda_k1, lambda_q2, lambda_k2, g_sb_out, g_df_out, w_out, g_ffn,
           w_router_group, b_router_group, w_router_expert, b_router_expert, w_expert_gate, w_expert_up,
           w_expert_down, g_ple, w_ple_gate, w_ple_proj, g_final):
    depth = p.shape[0]
    assert depth == 1, "the final norm is fused into the single layer's last kernel"
    return _layer(x, p[0], 0, g_mix[0], w_in[0], lambda_q1[0], lambda_k1[0], lambda_q2[0], lambda_k2[0],
                  g_sb_out[0], g_df_out[0], w_out[0], g_ffn[0], w_router_group[0], b_router_group[0],
                  w_router_expert[0], b_router_expert[0], w_expert_gate[0], w_expert_up[0],
                  w_expert_down[0], g_ple[0], w_ple_gate[0], w_ple_proj[0], g_final)
```

```python
import functools
import math

import jax
import jax.numpy as jnp
from jax import lax
from jax.experimental import pallas as pl
from jax.experimental.pallas import tpu as pltpu

F32 = jnp.float32
BF16 = jnp.bfloat16

HEAD_DIM = 64
LANES = 128
N_SB_PAIRS = 4
N_DF_HEADS = 4
SB_WIDTH = 512
DF_WIDTH = 512
SCALE = HEAD_DIM ** -0.5
NORM_EPS = 1e-6
N_GROUPS = 4
EXPERTS_PER_GROUP = 4
N_EXPERTS = 16
ROUTER_LANE0 = N_GROUPS
ALIBI_SLOPES = tuple(2.0 ** (-8.0 * (h + 1) / N_DF_HEADS) for h in range(N_DF_HEADS))
SB_LOG_ZERO = -110.0
VMEM_LIMIT = 48 * 1024 * 1024
MOE_TILE_ROWS = 512
META_ROWS = 8
ROW_GROUP = 8


def _rms(x, g):
    ms = jnp.mean(x * x, axis=-1, keepdims=True)
    return x * lax.rsqrt(ms + NORM_EPS) * g


def _dot(a, b):
    return jnp.dot(a, b, preferred_element_type=F32)


def _dot_nt(a, b):
    return lax.dot_general(a, b, (((1,), (1,)), ((), ())), preferred_element_type=F32)


def _rows_dot(dot, a, b, rows):
    return jnp.concatenate([dot(a[r:r + rows], b) for r in range(0, a.shape[0], rows)], axis=0)


def _params(*sem):
    return pltpu.CompilerParams(dimension_semantics=sem, vmem_limit_bytes=VMEM_LIMIT)


def _inproj_kernel(x_ref, g_ref, w_ref, o_ref, *, tn):
    h = _rms(x_ref[...], g_ref[...]).astype(BF16)
    for j in range(o_ref.shape[1] // tn):
        o_ref[:, j * tn:(j + 1) * tn] = _dot(h, w_ref[:, j * tn:(j + 1) * tn]).astype(o_ref.dtype)


def _inproj(x2d, g, w_bf16, tm):
    n, d = x2d.shape
    width = w_bf16.shape[1]
    return pl.pallas_call(
        functools.partial(_inproj_kernel, tn=1024),
        grid=(n // tm,),
        in_specs=[pl.BlockSpec((tm, d), lambda i: (i, 0)),
                  pl.BlockSpec((1, d), lambda i: (0, 0)),
                  pl.BlockSpec((d, width), lambda i: (0, 0))],
        out_specs=pl.BlockSpec((tm, width), lambda i: (i, 0)),
        out_shape=jax.ShapeDtypeStruct((n, width), BF16),
        compiler_params=_params("parallel"),
        name="inproj",
    )(x2d, g, w_bf16)


SB_STREAMS = 4


def _sb_kernel(q_ref, k_ref, v_ref, u_ref, o_ref, acc_ref, c_ref, z_ref, *, t):
    qi = pl.program_id(2)
    lane = lax.broadcasted_iota(jnp.int32, (t, LANES), 1)
    first = lane < HEAD_DIM

    def stacked_q(p):
        q = q_ref[0, :, p * LANES:(p + 1) * LANES] * SCALE
        zero = jnp.zeros_like(q)
        return jnp.concatenate([jnp.where(first, q, zero), jnp.where(first, zero, q)], axis=0)

    q2 = [stacked_q(p) for p in range(SB_STREAMS)]

    def logits(p, kb):
        k = k_ref[0, pl.ds(pl.multiple_of(kb * t, t), t), p * LANES:(p + 1) * LANES]
        return _rows_dot(_dot_nt, q2[p], k, t)

    def consume(p, z, kb, strict_mask):
        v = v_ref[0, pl.ds(pl.multiple_of(kb * t, t), t), p * LANES:(p + 1) * LANES]
        soft = jnp.log(1.0 + jnp.exp(-jnp.abs(z)))
        log_beta = jnp.minimum(z, 0.0) - soft
        log_keep = log_beta - z
        if strict_mask is not None:
            log_keep = jnp.where(strict_mask, log_keep, 0.0)
        hi = log_keep.astype(BF16)
        lo = (log_keep - hi.astype(F32)).astype(BF16)
        rev = _rows_dot(_dot, jnp.concatenate([hi, lo], axis=1), u_ref[...], t)
        c = c_ref[p]
        w = jnp.exp(log_beta + rev + jnp.concatenate([c] * (t // LANES), axis=1))
        if strict_mask is not None:
            w = jnp.where(strict_mask, w, 0.0)
        w = w.astype(BF16)
        vz = jnp.zeros_like(v)
        v2 = jnp.concatenate([jnp.where(first, v, vz), jnp.where(first, vz, v)], axis=0)
        acc_ref[p] += _dot(jnp.concatenate([w[:t], w[t:]], axis=1), v2)
        c_new = c + jnp.sum(log_keep, axis=-1, keepdims=True)
        c_ref[p] = c_new
        return jnp.max(c_new)

    def step(kb, strict_mask):
        kb_next = jnp.maximum(kb - 1, 0)
        worst = None
        for p in range(SB_STREAMS):
            z = z_ref[p]
            z_ref[p] = logits(p, kb_next)
            done = consume(p, z, kb, strict_mask)
            worst = done if worst is None else jnp.maximum(worst, done)
        return worst

    acc_ref[...] = jnp.zeros_like(acc_ref)
    c_ref[...] = jnp.zeros_like(c_ref)
    for p in range(SB_STREAMS):
        z_ref[p] = logits(p, qi)
    row = lax.broadcasted_iota(jnp.int32, (2 * t, t), 0)
    col = lax.broadcasted_iota(jnp.int32, (2 * t, t), 1)
    cmax = step(qi, col < jnp.where(row >= t, row - t, row))

    def cond(carry):
        kb, cm = carry
        return jnp.logical_and(kb >= 0, cm > SB_LOG_ZERO)

    def body(carry):
        kb, _ = carry
        return kb - 1, step(kb, None)

    lax.while_loop(cond, body, (qi - 1, cmax))
    for p in range(SB_STREAMS):
        o_ref[0, :, p * LANES:(p + 1) * LANES] = acc_ref[p]


def _sb_attention(proj3, u2, t):
    b, s, _ = proj3.shape
    groups = N_SB_PAIRS // SB_STREAMS
    width = SB_STREAMS * LANES
    return pl.pallas_call(
        functools.partial(_sb_kernel, t=t),
        grid=(b, groups, s // t),
        in_specs=[pl.BlockSpec((1, t, width), lambda bi, g, qi: (bi, qi, g)),
                  pl.BlockSpec((1, s, width), lambda bi, g, qi: (bi, 0, groups + g)),
                  pl.BlockSpec((1, s, width), lambda bi, g, qi: (bi, 0, 2 * groups + g)),
                  pl.BlockSpec((2 * t, t), lambda bi, g, qi: (0, 0))],
        out_specs=pl.BlockSpec((1, t, width), lambda bi, g, qi: (bi, qi, g)),
        out_shape=jax.ShapeDtypeStruct((b, s, SB_WIDTH), F32),
        scratch_shapes=[pltpu.VMEM((SB_STREAMS, t, LANES), F32), pltpu.VMEM((SB_STREAMS, 2 * t, LANES), F32),
                        pltpu.VMEM((SB_STREAMS, 2 * t, t), F32)],
        compiler_params=_params("parallel", "parallel", "arbitrary"),
        name="sb_attention",
    )(proj3, proj3, proj3, u2)


DF_STREAMS = 4


def _df_kernel(q_ref, k_ref, v_ref, lam_ref, g_ref, o_ref, acc_ref, m_ref, z_ref, *, t, lambda_init):
    for p in range(DF_STREAMS):
        _df_head(p, q_ref, k_ref, v_ref, lam_ref, g_ref, o_ref, acc_ref, m_ref, z_ref, t, lambda_init)


def _df_head(p, q_ref, k_ref, v_ref, lam_ref, g_ref, o_ref, acc_ref, m_ref, z_ref, t, lambda_init):
    h = pl.program_id(1) * DF_STREAMS + p
    lanes = slice(p * LANES, (p + 1) * LANES)
    qi = pl.program_id(2)
    slope = jnp.float32(ALIBI_SLOPES[-1])
    for idx in range(N_DF_HEADS - 2, -1, -1):
        slope = jnp.where(h == idx, jnp.float32(ALIBI_SLOPES[idx]), slope)
    lane = lax.broadcasted_iota(jnp.int32, (t, LANES), 1)
    row = lax.broadcasted_iota(jnp.int32, (t, LANES), 0)
    first = lane < HEAD_DIM
    q = q_ref[0, :, lanes] * SCALE
    zero = jnp.zeros_like(q)
    bias_on = jnp.where(lane < 2, 1.0, 0.0).astype(BF16)
    q2 = jnp.concatenate([jnp.concatenate([jnp.where(first, q, zero), bias_on], axis=1),
                          jnp.concatenate([jnp.where(first, zero, q), bias_on], axis=1)], axis=0)
    key_lo = jnp.where(lane == 0, ((row >> 8) << 8).astype(F32),
                       jnp.where(lane == 1, (row & 255).astype(F32), 0.0)) * slope
    lane0 = lane == 0
    ones_v = jnp.ones((t, LANES), BF16)

    def logits(kb):
        k = k_ref[0, pl.ds(pl.multiple_of(kb * t, t), t), lanes]
        offset = slope * ((kb - qi) * t).astype(F32)
        k_bias = (key_lo + jnp.where(lane0, offset, 0.0)).astype(BF16)
        return _dot_nt(q2, jnp.concatenate([k, k_bias], axis=1))

    def accumulate(z, kb):
        v = v_ref[0, pl.ds(pl.multiple_of(kb * t, t), t), lanes]
        m_old = m_ref[...]
        m_new = jnp.maximum(m_old, jnp.max(z, axis=-1, keepdims=True))
        alpha = jnp.exp(m_old - m_new)
        p = jnp.exp(z - jnp.concatenate([m_new] * (t // LANES), axis=1)).astype(BF16)
        pv = _dot(p, jnp.concatenate([v, ones_v], axis=1))
        acc_ref[...] = jnp.concatenate([alpha, alpha], axis=1) * acc_ref[...] + pv
        m_ref[...] = m_new

    acc_ref[...] = jnp.zeros_like(acc_ref)
    m_ref[...] = jnp.full_like(m_ref, -jnp.inf)
    r2 = lax.broadcasted_iota(jnp.int32, (2 * t, t), 0)
    c2 = lax.broadcasted_iota(jnp.int32, (2 * t, t), 1)
    z_ref[...] = jnp.where(c2 <= jnp.where(r2 >= t, r2 - t, r2), logits(qi), -jnp.inf)

    def body(j, carry):
        z = z_ref[...]
        z_ref[...] = logits(j)
        accumulate(z, jnp.where(j == 0, qi, j - 1))
        return carry

    lax.fori_loop(0, qi, body, 0)
    accumulate(z_ref[...], jnp.where(qi == 0, qi, qi - 1))

    lam_vec = lam_ref[...]
    lam = (jnp.exp(jnp.sum(lam_vec[0:1] * lam_vec[1:2], axis=-1, keepdims=True))
           - jnp.exp(jnp.sum(lam_vec[2:3] * lam_vec[3:4], axis=-1, keepdims=True))
           + lambda_init)
    acc = acc_ref[...]
    ratio = acc[:, :LANES] / acc[:, LANES:]
    o = ratio[:t] - lam * ratio[t:]
    o_ref[0, :, lanes] = (_rms(o, g_ref[...]) * (1.0 - lambda_init)).astype(o_ref.dtype)


def _df_attention(proj3, lam_rows, g_df, t, lambda_init):
    b, s, _ = proj3.shape
    groups = N_DF_HEADS // DF_STREAMS
    width = DF_STREAMS * LANES
    col0 = 3 * N_SB_PAIRS // DF_STREAMS
    return pl.pallas_call(
        functools.partial(_df_kernel, t=t, lambda_init=lambda_init),
        grid=(b, groups, s // t),
        in_specs=[pl.BlockSpec((1, t, width), lambda bi, g, qi: (bi, qi, col0 + g)),
                  pl.BlockSpec((1, s, width), lambda bi, g, qi: (bi, 0, col0 + groups + g)),
                  pl.BlockSpec((1, s, width), lambda bi, g, qi: (bi, 0, col0 + 2 * groups + g)),
                  pl.BlockSpec((4, HEAD_DIM), lambda bi, g, qi: (0, 0)),
                  pl.BlockSpec((1, LANES), lambda bi, g, qi: (0, 0))],
        out_specs=pl.BlockSpec((1, t, width), lambda bi, g, qi: (bi, qi, g)),
        out_shape=jax.ShapeDtypeStruct((b, s, DF_WIDTH), BF16),
        scratch_shapes=[pltpu.VMEM((2 * t, 2 * LANES), F32), pltpu.VMEM((2 * t, LANES), F32),
                        pltpu.VMEM((2 * t, t), F32)],
        compiler_params=_params("parallel", "parallel", "arbitrary"),
        name="df_attention",
    )(proj3, proj3, proj3, lam_rows, g_df)


def _split_bf16(x):
    hi = x.astype(BF16)
    return hi, (x - hi.astype(F32)).astype(BF16)


def _route(logits):
    lane = lax.broadcasted_iota(jnp.int32, logits.shape, 1)
    lane_f = lane.astype(F32)
    neg = -jnp.inf
    big = float(LANES)
    gl = jnp.where(lane < N_GROUPS, logits, neg)
    gmax = jnp.max(gl, axis=-1, keepdims=True)
    g_idx = jnp.min(jnp.where(gl == gmax, lane_f, big), axis=-1, keepdims=True)
    g_w = 1.0 / jnp.sum(jnp.exp(gl - gmax), axis=-1, keepdims=True)
    e_lane = lane - ROUTER_LANE0
    in_group = jnp.logical_and(jnp.logical_and(e_lane >= 0, e_lane < N_EXPERTS),
                               (e_lane // EXPERTS_PER_GROUP).astype(F32) == g_idx)
    v1 = jnp.where(in_group, logits, neg)
    t1 = jnp.max(v1, axis=-1, keepdims=True)
    i1 = jnp.min(jnp.where(v1 == t1, lane_f, big), axis=-1, keepdims=True)
    v2 = jnp.where(lane_f == i1, neg, v1)
    t2 = jnp.max(v2, axis=-1, keepdims=True)
    i2 = jnp.min(jnp.where(v2 == t2, lane_f, big), axis=-1, keepdims=True)
    e2 = jnp.exp(t2 - t1)
    w1 = g_w / (1.0 + e2)
    w2 = w1 * e2
    return jnp.where(lane == 0, i1 - ROUTER_LANE0, jnp.where(
        lane == 1, i2 - ROUTER_LANE0, jnp.where(lane == 2, w1, jnp.where(lane == 3, w2, 0.0))))


def _run_rows(route):
    lane_f = lax.broadcasted_iota(jnp.int32, route.shape, 1).astype(F32)
    onehot = jnp.where(lane_f == route[:, 0:1], 1.0, jnp.where(lane_f == route[:, 1:2], 1.0, 0.0))
    return jnp.ceil(jnp.sum(onehot, axis=0, keepdims=True) * (1.0 / ROW_GROUP)) * ROW_GROUP, onehot


def _outproj_kernel(osb_ref, odf_ref, x_ref, gsb_ref, wsb_ref, wdf_ref, gffn_ref, wr_ref, br_ref,
                    x1_ref, h2_ref, route_ref, runs_ref):
    a_sb = _rms(osb_ref[...], gsb_ref[...]).astype(BF16)
    x1 = x_ref[...] + _dot(a_sb, wsb_ref[...]) + _dot(odf_ref[...], wdf_ref[...])
    x1_ref[...] = x1
    h2 = _rms(x1, gffn_ref[...])
    h2_ref[...] = h2.astype(BF16)
    hi, lo = _split_bf16(h2)
    logits = _rows_dot(_dot, jnp.concatenate([hi, hi, lo], axis=1), wr_ref[...], hi.shape[0] // 2) + br_ref[...]
    route = _route(logits)
    route_ref[...] = route
    runs_ref[...] = jnp.broadcast_to(_run_rows(route)[0], runs_ref.shape)


def _outproj(o_sb, o_df, x2d, g_sb, w_sb, w_df, g_ffn, w_router3, b_router, tm):
    n, d = x2d.shape
    row = lambda i: (i, 0)
    const = lambda i: (0, 0)
    return pl.pallas_call(
        _outproj_kernel,
        grid=(n // tm,),
        in_specs=[pl.BlockSpec((tm, SB_WIDTH), row), pl.BlockSpec((tm, DF_WIDTH), row),
                  pl.BlockSpec((tm, d), row), pl.BlockSpec((1, SB_WIDTH), const),
                  pl.BlockSpec((SB_WIDTH, d), const), pl.BlockSpec((DF_WIDTH, d), const),
                  pl.BlockSpec((1, d), const), pl.BlockSpec((3 * d, LANES), const),
                  pl.BlockSpec((1, LANES), const)],
        out_specs=[pl.BlockSpec((tm, d), row), pl.BlockSpec((tm, d), row),
                   pl.BlockSpec((tm, LANES), row), pl.BlockSpec((META_ROWS, LANES), row)],
        out_shape=[jax.ShapeDtypeStruct((n, d), F32), jax.ShapeDtypeStruct((n, d), BF16),
                   jax.ShapeDtypeStruct((n, LANES), F32),
                   jax.ShapeDtypeStruct((n // tm * META_ROWS, LANES), F32)],
        compiler_params=_params("parallel"),
        name="outproj_router",
    )(o_sb, o_df, x2d, g_sb, w_sb, w_df, g_ffn, w_router3, b_router)


RUN_SIZES = (512, 256, 128, 64, 32, 16, 8)


def _lane_prefix(x_groups):
    r = lax.broadcasted_iota(jnp.int32, (LANES, LANES), 0)
    c = lax.broadcasted_iota(jnp.int32, (LANES, LANES), 1)
    return _dot(x_groups.astype(BF16), jnp.where(r < c, 1.0, 0.0).astype(BF16))


def _plan_kernel(route_ref, runs_ref, pos_ref, meta_ref, seg_ref, post_ref, run_ref, off_ref, *, tm, tile_rows):
    i = pl.program_id(0)
    route = route_ref[...]
    lane = lax.broadcasted_iota(jnp.int32, (tm, LANES), 1)
    lane_f = lane.astype(F32)
    sel1 = lane_f == route[:, 0:1]
    sel2 = lane_f == route[:, 1:2]
    run_rows, onehot = _run_rows(route)
    mrow = lax.broadcasted_iota(jnp.int32, (META_ROWS, LANES), 0)

    @pl.when(i == 0)
    def _():
        cnt = jnp.sum(runs_ref[...], axis=0, keepdims=True) * (1.0 / META_ROWS)
        cnt = jnp.broadcast_to(cnt, (META_ROWS, LANES))
        n_tiles = jnp.ceil(cnt * (1.0 / tile_rows))
        tile_off = _lane_prefix(n_tiles)
        off_ref[...] = tile_off * tile_rows
        run_ref[...] = jnp.zeros_like(run_ref)
        meta_ref[...] = jnp.where(mrow == 0, tile_off, jnp.where(mrow == 1, n_tiles,
                                                                 jnp.where(mrow == 2, cnt, 0.0)))

    r = lax.broadcasted_iota(jnp.int32, (tm, tm), 0)
    c = lax.broadcasted_iota(jnp.int32, (tm, tm), 1)
    earlier = _dot(jnp.where(c < r, 1.0, 0.0).astype(BF16), onehot.astype(BF16))
    local_start = _lane_prefix(jnp.broadcast_to(run_rows * (1.0 / ROW_GROUP), (META_ROWS, LANES))) * ROW_GROUP
    global_start = run_ref[...] + off_ref[...]
    base_g = earlier + global_start[0:1]
    base_l = earlier + local_start[0:1]

    def pick(sel, base):
        return jnp.sum(jnp.where(sel, base, 0.0), axis=-1, keepdims=True)

    pos = jnp.where(lane == 0, pick(sel1, base_g), jnp.where(
        lane == 1, pick(sel2, base_g), jnp.where(
            lane == 2, pick(sel1, base_l), jnp.where(lane == 3, pick(sel2, base_l), 0.0))))
    pos_ref[...] = pos.astype(jnp.int32)
    post_ref[...] = jnp.transpose(pos)[:META_ROWS].astype(jnp.int32)
    seg_ref[...] = jnp.where(mrow == 0, local_start, jnp.where(
        mrow == 1, run_rows, jnp.where(mrow == 2, global_start, 0.0))).astype(jnp.int32)
    run_ref[...] += run_rows


def _plan(route, runs, tm, tile_rows):
    n = route.shape[0]
    steps = n // tm
    assert (n + (ROW_GROUP - 1) * steps) // tile_rows + 1 <= 256, "per-expert tile counts must stay exact in bf16"
    assert (2 * tm) // ROW_GROUP + N_EXPERTS <= 256, "per-tile run sizes must stay exact in bf16"
    small = pltpu.VMEM((META_ROWS, LANES), F32)
    tile = lambda i: (i, 0)
    return pl.pallas_call(
        functools.partial(_plan_kernel, tm=tm, tile_rows=tile_rows),
        grid=(steps,),
        in_specs=[pl.BlockSpec((tm, LANES), tile),
                  pl.BlockSpec((steps * META_ROWS, LANES), lambda i: (0, 0))],
        out_specs=[pl.BlockSpec((tm, LANES), tile), pl.BlockSpec((META_ROWS, LANES), lambda i: (0, 0)),
                   pl.BlockSpec((META_ROWS, LANES), tile), pl.BlockSpec((META_ROWS, tm), tile)],
        out_shape=[jax.ShapeDtypeStruct((n, LANES), jnp.int32),
                   jax.ShapeDtypeStruct((META_ROWS, LANES), F32),
                   jax.ShapeDtypeStruct((steps * META_ROWS, LANES), jnp.int32),
                   jax.ShapeDtypeStruct((steps * META_ROWS, tm), jnp.int32)],
        scratch_shapes=[small, small],
        compiler_params=_params("arbitrary"),
        name="moe_plan",
    )(route, runs)


def _row_copy(src_ref, src_row, dst_ref, dst_row, sem):
    return pltpu.make_async_copy(src_ref.at[pl.ds(src_row, 1)], dst_ref.at[pl.ds(dst_row, 1)], sem)


def _for_each_run_piece(seg_ref, fn):
    def per_expert(e, carry):
        local = seg_ref[0, 0, e]
        length = seg_ref[0, 1, e]
        dst = seg_ref[0, 2, e]
        done = jnp.int32(0)
        for size in RUN_SIZES:
            has = (length & size) != 0

            @pl.when(has)
            def _(size=size, done=done):
                fn(pl.multiple_of(local + done, ROW_GROUP), pl.multiple_of(dst + done, ROW_GROUP), size)

            done = done + jnp.where(has, size, 0)
        return carry

    lax.fori_loop(0, N_EXPERTS, per_expert, 0)


def _dispatch_kernel(meta_ref, h_ref, pos_ref, seg_ref, seg_prev_ref, xs_ref, local_ref, zero_ref, sem, fill_sem,
                     *, tm, tile_rows):
    i = pl.program_id(0)
    last = pl.num_programs(0) - 1
    buf = lax.rem(i, 2)
    rows = local_ref.shape[1]
    local_pos = jnp.transpose(pos_ref[...].astype(F32))
    lp1 = local_pos[2:3].astype(jnp.int32)
    lp2 = local_pos[3:4].astype(jnp.int32)
    r = lax.broadcasted_iota(jnp.int32, (rows, tm), 0)
    place = jnp.where(r == lp1, 1.0, jnp.where(r == lp2, 1.0, 0.0)).astype(BF16)
    local_ref[buf] = _dot(place, h_ref[...])

    def run_copy(b):
        def make(local, dst, size):
            return pltpu.make_async_copy(local_ref.at[b, pl.ds(local, size)], xs_ref.at[pl.ds(dst, size)],
                                         sem.at[b])
        return make

    _for_each_run_piece(seg_ref, lambda local, dst, size: run_copy(buf)(local, dst, size).start())

    @pl.when(i > 0)
    def _():
        _for_each_run_piece(seg_prev_ref, lambda local, dst, size: run_copy(1 - buf)(local, dst, size).wait())

    @pl.when(i == last)
    def _():
        _for_each_run_piece(seg_ref, lambda local, dst, size: run_copy(buf)(local, dst, size).wait())
        zero_ref[...] = jnp.zeros_like(zero_ref)

        def per_expert(e, carry):
            off = meta_ref[0, e] * tile_rows
            lo = meta_ref[2, e]
            hi = meta_ref[1, e] * tile_rows

            def fill(g, c):
                pltpu.make_async_copy(zero_ref.at[pl.ds(0, ROW_GROUP)],
                                      xs_ref.at[pl.ds(pl.multiple_of(off + g * ROW_GROUP, ROW_GROUP), ROW_GROUP)],
                                      fill_sem).start()
                return c

            def drain(g, c):
                pltpu.make_async_copy(zero_ref.at[pl.ds(0, ROW_GROUP)], xs_ref.at[pl.ds(0, ROW_GROUP)],
                                      fill_sem).wait()
                return c

            lax.fori_loop(lo // ROW_GROUP, hi // ROW_GROUP, fill, 0)
            lax.fori_loop(lo // ROW_GROUP, hi // ROW_GROUP, drain, 0)
            return carry

        lax.fori_loop(0, N_EXPERTS, per_expert, 0)

        def spare_tile(j, carry):
            c = pltpu.make_async_copy(zero_ref, xs_ref.at[pl.ds(j * tile_rows, tile_rows)], fill_sem)
            c.start()
            c.wait()
            return carry

        n_used = meta_ref[0, N_EXPERTS - 1] + meta_ref[1, N_EXPERTS - 1]
        lax.fori_loop(n_used, xs_ref.shape[0] // tile_rows, spare_tile, 0)


def _dispatch(meta_i, h2, pos, seg3, n_rows, tm, tile_rows):
    n, d = h2.shape
    local_rows = 2 * tm + N_EXPERTS * ROW_GROUP
    seg_spec = lambda index_map: pl.BlockSpec((1, META_ROWS, LANES), index_map, memory_space=pltpu.SMEM)
    return pl.pallas_call(
        functools.partial(_dispatch_kernel, tm=tm, tile_rows=tile_rows),
        grid_spec=pltpu.PrefetchScalarGridSpec(
            num_scalar_prefetch=1,
            grid=(n // tm,),
            in_specs=[pl.BlockSpec((tm, d), lambda i, meta: (i, 0)),
                      pl.BlockSpec((tm, LANES), lambda i, meta: (i, 0)),
                      seg_spec(lambda i, meta: (i, 0, 0)),
                      seg_spec(lambda i, meta: (jnp.maximum(i - 1, 0), 0, 0))],
            out_specs=pl.BlockSpec(memory_space=pl.ANY),
            scratch_shapes=[pltpu.VMEM((2, local_rows, d), F32), pltpu.VMEM((tile_rows, d), F32),
                            pltpu.SemaphoreType.DMA((2,)), pltpu.SemaphoreType.DMA(())]),
        out_shape=jax.ShapeDtypeStruct((n_rows, d), F32),
        compiler_params=_params("arbitrary"),
        name="moe_dispatch",
    )(meta_i, h2, pos, seg3, seg3)


def _gmm_kernel(te_ref, nv_ref, x_ref, wg_ref, wu_ref, wd_ref, y_ref, wg_bf, wu_bf, wd_bf):
    j = pl.program_id(0)
    used = j < nv_ref[0]
    new_expert = jnp.logical_or(j == 0, te_ref[j] != te_ref[jnp.maximum(j - 1, 0)])

    @pl.when(jnp.logical_and(used, new_expert))
    def _():
        wg_bf[...] = wg_ref[0].astype(BF16)
        wu_bf[...] = wu_ref[0].astype(BF16)
        wd_bf[...] = wd_ref[0].astype(BF16)

    @pl.when(used)
    def _():
        x = x_ref[...].astype(BF16)
        a = _dot(x, wg_bf[...])
        hid = (a * jax.nn.sigmoid(a)) * _dot(x, wu_bf[...])
        y_ref[...] = _dot(hid.astype(BF16), wd_bf[...])

    @pl.when(jnp.logical_not(used))
    def _():
        y_ref[...] = jnp.zeros_like(y_ref)


def _gmm(tile_expert, n_valid, xs, wg, wu, wd, tile_rows):
    n_rows, d = xs.shape
    de = wg.shape[2]
    rows = lambda j, te, nv: (jnp.minimum(j, nv[0] - 1), 0)
    return pl.pallas_call(
        _gmm_kernel,
        grid_spec=pltpu.PrefetchScalarGridSpec(
            num_scalar_prefetch=2,
            grid=(n_rows // tile_rows,),
            in_specs=[pl.BlockSpec((tile_rows, d), rows),
                      pl.BlockSpec((1, d, de), lambda j, te, nv: (te[j], 0, 0)),
                      pl.BlockSpec((1, d, de), lambda j, te, nv: (te[j], 0, 0)),
                      pl.BlockSpec((1, de, d), lambda j, te, nv: (te[j], 0, 0))],
            out_specs=pl.BlockSpec((tile_rows, d), lambda j, te, nv: (j, 0)),
            scratch_shapes=[pltpu.VMEM((d, de), BF16), pltpu.VMEM((d, de), BF16), pltpu.VMEM((de, d), BF16)]),
        out_shape=jax.ShapeDtypeStruct((n_rows, d), F32),
        compiler_params=_params("arbitrary"),
        name="moe_experts",
    )(tile_expert, n_valid, xs, wg, wu, wd)


def _ple_kernel(x1_ref, p_ref, route_ref, pos0_ref, pos1_ref, pos0_next_ref, pos1_next_ref, ys_ref, gple_ref,
                wgate_ref, wproj_ref, gfin_ref, o_ref, y_ref, sem, *, tm):
    i = pl.program_id(0)
    buf = lax.rem(i, 2)

    def gather(pos, b):
        def start(g, carry):
            base = pl.multiple_of(g * ROW_GROUP, ROW_GROUP)
            for j in range(ROW_GROUP):
                for slot in range(2):
                    src = pos[slot][0, 0, base + j]
                    _row_copy(ys_ref, src, y_ref.at[b, slot], base + j, sem.at[b]).start(priority=slot)
            return carry

        lax.fori_loop(0, tm // ROW_GROUP, start, 0)

    @pl.when(i == 0)
    def _():
        gather((pos0_ref, pos1_ref), 0)

    @pl.when(i + 1 < pl.num_programs(0))
    def _():
        gather((pos0_next_ref, pos1_next_ref), 1 - buf)

    def wait(r, carry):
        _row_copy(ys_ref, 0, y_ref.at[buf, 0], 0, sem.at[buf]).wait()
        return carry

    emb = _dot(p_ref[...].astype(BF16), wproj_ref[...])
    lax.fori_loop(0, 2 * tm, wait, 0, unroll=8)
    route = route_ref[...]
    x = x1_ref[...] + (route[:, 2:3] * y_ref[buf, 0] + route[:, 3:4] * y_ref[buf, 1])
    gate = jax.nn.sigmoid(_dot(_rms(x, gple_ref[...]).astype(BF16), wgate_ref[...]))
    o_ref[...] = _rms(x + gate * emb, gfin_ref[...])


def _ple(x1, p2d, route, pos_slots, ys, g_ple, w_gate, w_proj, g_final, tm):
    n, d = x1.shape
    pd = p2d.shape[1]
    steps = n // tm
    row = lambda i: (i, 0)
    const = lambda i: (0, 0)
    pos_spec = lambda index_map: pl.BlockSpec((1, 1, tm), index_map, memory_space=pltpu.SMEM)
    here = lambda i: (i, 0, 0)
    ahead = lambda i: (jnp.minimum(i + 1, steps - 1), 0, 0)
    return pl.pallas_call(
        functools.partial(_ple_kernel, tm=tm),
        grid=(steps,),
        in_specs=[pl.BlockSpec((tm, d), row), pl.BlockSpec((tm, pd), row), pl.BlockSpec((tm, LANES), row),
                  pos_spec(here), pos_spec(here), pos_spec(ahead), pos_spec(ahead),
                  pl.BlockSpec(memory_space=pl.ANY), pl.BlockSpec((1, d), const),
                  pl.BlockSpec((d, d), const), pl.BlockSpec((pd, d), const), pl.BlockSpec((1, d), const)],
        out_specs=pl.BlockSpec((tm, d), row),
        out_shape=jax.ShapeDtypeStruct((n, d), F32),
        scratch_shapes=[pltpu.VMEM((2, 2, tm, d), F32), pltpu.SemaphoreType.DMA((2,))],
        compiler_params=_params("arbitrary"),
        name="combine_ple_final",
    )(x1, p2d, route, *pos_slots, *pos_slots, ys, g_ple, w_gate, w_proj, g_final)


def _layer(x, p_i, layer_idx, g_mix, w_in, lambda_q1, lambda_k1, lambda_q2, lambda_k2, g_sb_out, g_df_out,
           w_out, g_ffn, w_router_group, b_router_group, w_router_expert, b_router_expert, w_expert_gate,
           w_expert_up, w_expert_down, g_ple, w_ple_gate, w_ple_proj, g_out):
    b, s, d = x.shape
    n = b * s
    tm = min(512, n)
    t = min(256, s)
    x2d = x.reshape(n, d)
    lambda_init = 0.8 - 0.6 * math.exp(-0.3 * layer_idx)

    proj = _inproj(x2d, g_mix.reshape(1, d), w_in.astype(BF16), tm)
    proj3 = proj.reshape(b, s, proj.shape[1])

    tri = (lax.broadcasted_iota(jnp.int32, (t, t), 0) > lax.broadcasted_iota(jnp.int32, (t, t), 1))
    u2 = jnp.concatenate([tri, tri], axis=0).astype(BF16)
    o_sb = _sb_attention(proj3, u2, t).reshape(n, SB_WIDTH)

    lam_rows = jnp.stack([lambda_q1, lambda_k1, lambda_q2, lambda_k2]).astype(F32)
    o_df = _df_attention(proj3, lam_rows, g_df_out.reshape(1, LANES).astype(F32), min(512, s),
                         lambda_init).reshape(n, DF_WIDTH)

    w_router = jnp.concatenate([w_router_group, w_router_expert], axis=1).astype(F32)
    w_router = jnp.pad(w_router, ((0, 0), (0, LANES - w_router.shape[1])))
    wr_hi, wr_lo = _split_bf16(w_router)
    b_router = jnp.pad(jnp.concatenate([b_router_group, b_router_expert]).astype(F32),
                       (0, LANES - N_GROUPS - N_EXPERTS)).reshape(1, LANES)
    w_out_bf = w_out.astype(BF16)
    x1, h2, route, runs = _outproj(o_sb, o_df, x2d, g_sb_out.reshape(1, SB_WIDTH), w_out_bf[:SB_WIDTH],
                             w_out_bf[SB_WIDTH:], g_ffn.reshape(1, d),
                             jnp.concatenate([wr_hi, wr_lo, wr_hi], axis=0), b_router, tm)

    tile_rows = min(MOE_TILE_ROWS, n)
    pos, meta, seg, pos_t = _plan(route, runs, tm, tile_rows)
    run_padding = (n // tm) * N_EXPERTS * ROW_GROUP
    n_tiles = pl.cdiv(2 * n + run_padding, tile_rows) + N_EXPERTS
    meta_i = meta[:3, :N_EXPERTS].astype(jnp.int32)
    ends = meta_i[0] + meta_i[1]
    n_valid = ends[-1:]
    tile_ids = jnp.arange(n_tiles, dtype=jnp.int32)
    tile_expert = jnp.sum(tile_ids[:, None] >= ends[None, :], axis=1).astype(jnp.int32)
    tile_expert = jnp.where(tile_ids < n_valid, tile_expert, tile_expert[n_valid[0] - 1])
    xs = _dispatch(meta_i, h2, pos, seg.reshape(n // tm, META_ROWS, LANES), n_tiles * tile_rows, tm, tile_rows)
    ys = _gmm(tile_expert, n_valid, xs, w_expert_gate, w_expert_up, w_expert_down, tile_rows)

    pos_t3 = pos_t.reshape(n // tm, META_ROWS, tm)
    out = _ple(x1, p_i.reshape(n, p_i.shape[-1]), route, (pos_t3[:, 0:1], pos_t3[:, 1:2]), ys,
               g_ple.reshape(1, d), w_ple_gate.astype(BF16), w_ple_proj.astype(BF16), g_out.reshape(1, d), tm)
    return out.reshape(b, s, d)


def kernel(x, p, g_mix, w_in, lambda_q1, lambda_k1, lambda_q2, lambda_k2, g_sb_out, g_df_out, w_out, g_ffn,
           w_router_group, b_router_group, w_router_expert, b_router_expert, w_expert_gate, w_expert_up,
           w_expert_down, g_ple, w_ple_gate, w_ple_proj, g_final):
    depth = p.shape[0]
    assert depth == 1, "the final norm is fused into the single layer's last kernel"
    return _layer(x, p[0], 0, g_mix[0], w_in[0], lambda_q1[0], lambda_k1[0], lambda_q2[0], lambda_k2[0],
                  g_sb_out[0], g_df_out[0], w_out[0], g_ffn[0], w_router_group[0], b_router_group[0],
                  w_router_expert[0], b_router_expert[0], w_expert_gate[0], w_expert_up[0],
                  w_expert_down[0], g_ple[0], w_ple_gate[0], w_ple_proj[0], g_final)
```
